```python
import jax, jax.numpy as jnp
from jax import lax
import numpy as np

D_MODEL = 1024
BATCH = 16
SEQ = 2048
DEPTH = 1

GRID_W = 64
CTX_LEN = 256
CHUNK = 64
EPS = 1e-6
N_MOD = 6

SSD_HEAD_DIM = 64
SSD_D_INNER = D_MODEL
SSD_HEADS = SSD_D_INNER // SSD_HEAD_DIM
SSD_GROUPS = 4
SSD_STATE = 128
SSD_CONV_W = 3
SSD_XBC = SSD_D_INNER + 2 * SSD_GROUPS * SSD_STATE

GLA_HEADS = 4
GLA_KEY_DIM = D_MODEL // 2
GLA_VAL_DIM = D_MODEL
GLA_DK = GLA_KEY_DIM // GLA_HEADS
GLA_DV = GLA_VAL_DIM // GLA_HEADS
GLA_GATE_RANK = 16
GLA_GATE_NORM = 16.0

D_FF = 256 * ((8 * D_MODEL // 3 + 255) // 256)
FFN_CONV_W = 3

IN_SIZES = (SSD_D_INNER, SSD_XBC, 2 * SSD_HEADS, GLA_KEY_DIM, GLA_KEY_DIM, GLA_VAL_DIM, GLA_GATE_RANK, GLA_VAL_DIM)
IN_DIM = sum(IN_SIZES)

kernel_name = "hybrid_ssd_gla_convffn_dit"


def rmsnorm(x, g):
    xf = x.astype(jnp.float32)
    y = xf * lax.rsqrt(jnp.mean(xf * xf, axis=-1, keepdims=True) + EPS)
    return y.astype(x.dtype) * g


def modulate(x, shift, scale):
    return x * (1.0 + scale) + shift


def adaln(cond, w, b):
    mod = jax.nn.silu(cond) @ w + b
    return mod.reshape(cond.shape[0], N_MOD, 1, D_MODEL)


def _rev(t):
    return jnp.flip(t, axis=1)


def dwconv1d(x, w, b):
    width = w.shape[0]
    pad = width // 2
    L = x.shape[1]
    xp = jnp.pad(x, ((0, 0), (pad, pad), (0, 0)))
    out = xp[:, 0:L] * w[0]
    for i in range(1, width):
        out = out + xp[:, i:i + L] * w[i]
    return out + b


def chunk_recurrence(decay, states, init):
    def step(h, inp):
        d, s = inp
        return d * h + s, h
    h_final, h_prev = lax.scan(step, init, (jnp.moveaxis(decay, 1, 0), jnp.moveaxis(states, 1, 0)))
    return jnp.moveaxis(h_prev, 0, 1), h_final


def ssd_scan(x, dt, a, bmat, cmat, init, with_output):
    bsz, L, H, P = x.shape
    G, N = bmat.shape[-2:]
    R = H // G
    nc = L // CHUNK
    f32 = jnp.float32
    dt = dt.astype(f32)
    xd = (x.astype(f32) * dt[..., None]).reshape(bsz, nc, CHUNK, G, R, P)
    da = (dt * a.astype(f32)).reshape(bsz, nc, CHUNK, G, R)
    bm = bmat.astype(f32).reshape(bsz, nc, CHUNK, G, N)
    cm = cmat.astype(f32).reshape(bsz, nc, CHUNK, G, N)
    cum = jnp.cumsum(da, axis=2)
    last = cum[:, :, -1]
    states = jnp.einsum("bcsgn,bcsgr,bcsgrp->bcgrpn", bm, jnp.exp(last[:, :, None] - cum), xd)
    if init is None:
        init = jnp.zeros((bsz, G, R, P, N), f32)
    h_prev, h_final = chunk_recurrence(jnp.exp(last)[..., None, None], states, init)
    if not with_output:
        return None, h_final
    tri = jnp.tril(jnp.ones((CHUNK, CHUNK), bool))[:, :, None, None]
    seg = cum[:, :, :, None] - cum[:, :, None, :]
    decay = jnp.exp(jnp.where(tri, seg, -jnp.inf))
    cb = jnp.einsum("bcqgn,bcsgn->bcqsg", cm, bm)
    y = (jnp.einsum("bcqsg,bcqsgr,bcsgrp->bcqgrp", cb, decay, xd)
         + jnp.einsum("bcqgn,bcqgr,bcgrpn->bcqgrp", cm, jnp.exp(cum), h_prev))
    return y.reshape(bsz, L, H, P).astype(x.dtype), h_final


def gla_scan(q, k, v, log_a, init, with_output):
    bsz, L, H, K = q.shape
    V = v.shape[-1]
    nc = L // CHUNK
    f32 = jnp.float32
    qc = q.astype(f32).reshape(bsz, nc, CHUNK, H, K)
    kc = k.astype(f32).reshape(bsz, nc, CHUNK, H, K)
    vc = v.astype(f32).reshape(bsz, nc, CHUNK, H, V)
    cum = jnp.cumsum(log_a.astype(f32).reshape(bsz, nc, CHUNK, H, K), axis=2)
    last = cum[:, :, -1]
    states = jnp.einsum("bcshk,bcshv->bchkv", kc * jnp.exp(last[:, :, None] - cum), vc)
    if init is None:
        init = jnp.zeros((bsz, H, K, V), f32)
    h_prev, h_final = chunk_recurrence(jnp.exp(last)[..., None], states, init)
    if not with_output:
        return None, h_final
    ref = cum[:, :, CHUNK // 2 - 1:CHUNK // 2]
    scores = jnp.einsum("bcqhk,bcshk->bchqs", qc * jnp.exp(cum - ref), kc * jnp.exp(ref - cum))
    tri = jnp.tril(jnp.ones((CHUNK, CHUNK), bool))
    scores = jnp.where(tri, scores, 0.0)
    o = (jnp.einsum("bchqs,bcshv->bcqhv", scores, vc)
         + jnp.einsum("bcqhk,bchkv->bcqhv", qc * jnp.exp(cum), h_prev))
    return o.reshape(bsz, L, H, V).astype(v.dtype), h_final


def token_mixer(h, init, with_output, w_in, conv_w, conv_b, dt_bias, a_log, d_skip, ssd_norm_g,
                gla_gate_w, gla_gate_b, gla_norm_g, w_br_ssd, w_br_gla, w_merge, b_merge, w_o):
    bsz, L, _ = h.shape
    proj = h @ w_in
    z, xbc, dt_raw, q, k, v, g_lr, r = jnp.split(proj, np.cumsum(IN_SIZES)[:-1].tolist(), axis=-1)
    s0 = (None, None, None, None) if init is None else init

    xbc = jax.nn.silu(dwconv1d(xbc, conv_w, conv_b))
    xs, bm, cm = jnp.split(xbc, [SSD_D_INNER, SSD_D_INNER + SSD_GROUPS * SSD_STATE], axis=-1)
    xs = xs.reshape(bsz, L, SSD_HEADS, SSD_HEAD_DIM)
    bm = bm.reshape(bsz, L, SSD_GROUPS, SSD_STATE)
    cm = cm.reshape(bsz, L, SSD_GROUPS, SSD_STATE)
    dt = jax.nn.softplus(dt_raw.reshape(bsz, L, 2, SSD_HEADS) + dt_bias)
    a = -jnp.exp(a_log.astype(jnp.float32))
    y_f, sf = ssd_scan(xs, dt[:, :, 0], a[0], bm, cm, s0[0], with_output)
    y_b, sb = ssd_scan(_rev(xs), _rev(dt[:, :, 1]), a[1], _rev(bm), _rev(cm), s0[1], with_output)

    q = q.reshape(bsz, L, GLA_HEADS, GLA_DK) * (GLA_DK ** -0.5)
    k = k.reshape(bsz, L, GLA_HEADS, GLA_DK)
    v = v.reshape(bsz, L, GLA_HEADS, GLA_DV)
    log_a = jax.nn.log_sigmoid(jnp.einsum("blr,drk->bldk", g_lr, gla_gate_w) + gla_gate_b) / GLA_GATE_NORM
    log_a = log_a.reshape(bsz, L, 2, GLA_HEADS, GLA_DK)
    o_f, gf = gla_scan(q, k, v, log_a[:, :, 0], s0[2], with_output)
    o_b, gb = gla_scan(_rev(q), _rev(k), _rev(v), _rev(log_a[:, :, 1]), s0[3], with_output)
    states = (sf, sb, gf, gb)
    if not with_output:
        return None, states

    y = y_f + _rev(y_b) + d_skip[:, None] * xs
    y = rmsnorm(y.reshape(bsz, L, SSD_D_INNER) * jax.nn.silu(z), ssd_norm_g)
    o = rmsnorm(o_f + _rev(o_b), gla_norm_g).reshape(bsz, L, GLA_VAL_DIM) * jax.nn.silu(r)

    gates = jax.nn.sigmoid(h @ w_merge + b_merge)
    g_ssd, g_gla = jnp.split(gates, 2, axis=-1)
    out = (g_ssd * (y @ w_br_ssd) + g_gla * (o @ w_br_gla)) @ w_o
    return out, states


def conv_ffn(h, w_up, conv_w, conv_b, w_down, rows):
    u = h @ w_up
    if rows is None:
        u = dwconv1d(u, conv_w[FFN_CONV_W // 2], conv_b)
    else:
        bsz, L, ch = u.shape
        grid = u.reshape(bsz, rows, GRID_W, ch)
        grid = lax.conv_general_dilated(grid, conv_w[:, :, None, :], (1, 1), "SAME",
                                        dimension_numbers=("NHWC", "HWIO", "NHWC"),
                                        feature_group_count=ch)
        u = grid.reshape(bsz, L, ch) + conv_b
    gate, val = jnp.split(u, 2, axis=-1)
    return (jax.nn.silu(gate) * val) @ w_down


def setup_inputs(seed: int = 0) -> dict:
    key = jax.random.key(seed)
    ks = jax.random.split(key, 32)
    f32 = jnp.float32
    nrm = lambda k, shape, s: jax.random.normal(k, shape, f32) * s
    gain = lambda k, shape: 1.0 + 0.1 * jax.random.normal(k, shape, f32)
    dt0 = jnp.exp(jax.random.uniform(ks[8], (DEPTH, 2, SSD_HEADS), f32) * (np.log(0.1) - np.log(0.001)) + np.log(0.001))
    return {
        "x": nrm(ks[0], (BATCH, SEQ, D_MODEL), 1.0),
        "c": nrm(ks[1], (BATCH, D_MODEL), 1.0),
        "ctx": nrm(ks[2], (BATCH, CTX_LEN, D_MODEL), 1.0),
        "c_ctx": nrm(ks[3], (D_MODEL,), 1.0),
        "w_ada": nrm(ks[4], (DEPTH, D_MODEL, N_MOD * D_MODEL), 0.5 * D_MODEL ** -0.5),
        "b_ada": nrm(ks[5], (DEPTH, N_MOD * D_MODEL), 0.02),
        "norm1_g": gain(ks[6], (DEPTH, D_MODEL)),
        "w_in": nrm(ks[7], (DEPTH, D_MODEL, IN_DIM), D_MODEL ** -0.5),
        "ssd_conv_w": nrm(ks[9], (DEPTH, SSD_CONV_W, SSD_XBC), SSD_CONV_W ** -0.5),
        "ssd_conv_b": nrm(ks[10], (DEPTH, SSD_XBC), 0.02),
        "ssd_dt_bias": dt0 + jnp.log(-jnp.expm1(-dt0)),
        "ssd_a_log": jnp.log(jax.random.uniform(ks[11], (DEPTH, 2, SSD_HEADS), f32, 1.0, 16.0)),
        "ssd_d": gain(ks[12], (DEPTH, SSD_HEADS)),
        "ssd_norm_g": gain(ks[13], (DEPTH, SSD_D_INNER)),
        "gla_gate_w": nrm(ks[14], (DEPTH, 2, GLA_GATE_RANK, GLA_KEY_DIM), GLA_GATE_RANK ** -0.5),
        "gla_gate_b": nrm(ks[15], (DEPTH, 2, GLA_KEY_DIM), 0.1),
        "gla_norm_g": gain(ks[16], (DEPTH, GLA_DV)),
        "w_br_ssd": nrm(ks[17], (DEPTH, SSD_D_INNER, D_MODEL), SSD_D_INNER ** -0.5),
        "w_br_gla": nrm(ks[18], (DEPTH, GLA_VAL_DIM, D_MODEL), GLA_VAL_DIM ** -0.5),
        "w_merge": nrm(ks[19], (DEPTH, D_MODEL, 2 * D_MODEL), D_MODEL ** -0.5),
        "b_merge": nrm(ks[20], (DEPTH, 2 * D_MODEL), 0.02),
        "w_o": nrm(ks[21], (DEPTH, D_MODEL, D_MODEL), D_MODEL ** -0.5),
        "norm2_g": gain(ks[22], (DEPTH, D_MODEL)),
        "w_up": nrm(ks[23], (DEPTH, D_MODEL, 2 * D_FF), D_MODEL ** -0.5),
        "ffn_conv_w": nrm(ks[24], (DEPTH, FFN_CONV_W, FFN_CONV_W, 2 * D_FF), 1.0 / FFN_CONV_W),
        "ffn_conv_b": nrm(ks[25], (DEPTH, 2 * D_FF), 0.02),
        "w_down": nrm(ks[26], (DEPTH, D_FF, D_MODEL), D_FF ** -0.5),
        "final_norm_g": gain(ks[27], (D_MODEL,)),
    }


def reference(x, c, ctx, c_ctx, w_ada, b_ada, norm1_g, w_in, ssd_conv_w, ssd_conv_b, ssd_dt_bias,
              ssd_a_log, ssd_d, ssd_norm_g, gla_gate_w, gla_gate_b, gla_norm_g, w_br_ssd, w_br_gla,
              w_merge, b_merge, w_o, norm2_g, w_up, ffn_conv_w, ffn_conv_b, w_down, final_norm_g):
    rows = x.shape[1] // GRID_W
    for l in range(DEPTH):
        last = l == DEPTH - 1
        lp = dict(w_in=w_in[l], conv_w=ssd_conv_w[l], conv_b=ssd_conv_b[l], dt_bias=ssd_dt_bias[l],
                  a_log=ssd_a_log[l], d_skip=ssd_d[l], ssd_norm_g=ssd_norm_g[l],
                  gla_gate_w=gla_gate_w[l], gla_gate_b=gla_gate_b[l], gla_norm_g=gla_norm_g[l],
                  w_br_ssd=w_br_ssd[l], w_br_gla=w_br_gla[l], w_merge=w_merge[l], b_merge=b_merge[l],
                  w_o=w_o[l])
        mx = adaln(c, w_ada[l], b_ada[l])
        mc = adaln(c_ctx[None], w_ada[l], b_ada[l])

        h_ctx = modulate(rmsnorm(ctx, norm1_g[l]), mc[:, 0], mc[:, 1])
        out_ctx, ctx_states = token_mixer(h_ctx, None, not last, **lp)

        h_x = modulate(rmsnorm(x, norm1_g[l]), mx[:, 0], mx[:, 1])
        out_x, _ = token_mixer(h_x, ctx_states, True, **lp)
        x = x + mx[:, 2] * out_x
        h2 = modulate(rmsnorm(x, norm2_g[l]), mx[:, 3], mx[:, 4])
        x = x + mx[:, 5] * conv_ffn(h2, w_up[l], ffn_conv_w[l], ffn_conv_b[l], w_down[l], rows)

        if not last:
            ctx = ctx + mc[:, 2] * out_ctx
            h2c = modulate(rmsnorm(ctx, norm2_g[l]), mc[:, 3], mc[:, 4])
            ctx = ctx + mc[:, 5] * conv_ffn(h2c, w_up[l], ffn_conv_w[l], ffn_conv_b[l], w_down[l], None)
    return rmsnorm(x, final_norm_g)
```

```python
import functools

import numpy as np
import jax
import jax.numpy as jnp
from jax import lax
from jax.experimental import pallas as pl
from jax.experimental.pallas import tpu as pltpu

F32 = jnp.float32
BF16 = jnp.bfloat16

D_MODEL = 1024
GRID_W = 64
CHUNK = 64
EPS = 1e-6
N_MOD = 6

SSD_HEAD_DIM = 64
SSD_HEADS = 16
SSD_GROUPS = 4
SSD_STATE = 128
SSD_GROUP_W = (SSD_HEADS // SSD_GROUPS) * SSD_HEAD_DIM
SSD_D_INNER = D_MODEL

GLA_HEADS = 4
GLA_DK = 128
GLA_DV = 256
GLA_GATE_RANK = 16
GLA_GATE_NORM = 16.0
GLA_KEY_DIM = GLA_HEADS * GLA_DK

D_FF = 2816
SMALL_W = 128
DT_COLS = 2 * SSD_HEADS
NEG_BIG = -1e30

VMEM_LIMIT_BYTES = 56 * 1024 * 1024


def _cparams(n_grid):
    return pltpu.CompilerParams(
        dimension_semantics=("arbitrary",) * n_grid,
        vmem_limit_bytes=VMEM_LIMIT_BYTES,
    )


def _const_spec(shape):
    nd = len(shape)
    return pl.BlockSpec(tuple(shape), lambda *_: (0,) * nd, pipeline_mode=pl.Buffered(1))


def _silu(v):
    return v * jax.nn.sigmoid(v)


def _softplus(v):
    return jnp.maximum(v, 0.0) + jnp.log1p(jnp.exp(-jnp.abs(v)))


def _split3(v):
    hi = v.astype(BF16)
    r1 = v - hi.astype(F32)
    mid = r1.astype(BF16)
    lo = (r1 - mid.astype(F32)).astype(BF16)
    return hi, mid, lo


def _dot(a, b):
    return jnp.dot(a, b, preferred_element_type=F32)


def _dot_tn(a, b):
    return lax.dot_general(a, b, (((0,), (0,)), ((), ())), preferred_element_type=F32)


def _dot_nt(a, b):
    return lax.dot_general(a, b, (((1,), (1,)), ((), ())), preferred_element_type=F32)


def _adaln_kernel(c_ref, w_ref, b_ref, o_ref):
    s = _silu(c_ref[...])
    o_ref[...] = jnp.dot(s, w_ref[...], preferred_element_type=F32,
                         precision=lax.Precision.HIGHEST) + b_ref[...]


def _adaln(cond, w, b):
    rows = cond.shape[0]
    n_out = w.shape[1]
    tn = D_MODEL
    return pl.pallas_call(
        _adaln_kernel,
        out_shape=jax.ShapeDtypeStruct((rows, n_out), F32),
        grid=(n_out // tn,),
        in_specs=[
            pl.BlockSpec((rows, D_MODEL), lambda j: (0, 0)),
            pl.BlockSpec((D_MODEL, tn), lambda j: (0, j)),
            pl.BlockSpec((1, tn), lambda j: (0, j)),
        ],
        out_specs=pl.BlockSpec((rows, tn), lambda j: (0, j)),
        compiler_params=_cparams(1),
        name="adaln",
    )(cond, w, b)


def _inproj_kernel(posts, x_ref, shift_ref, scale_ref, g_ref, bias_ref, *refs):
    n = len(posts)
    w_refs, o_refs = refs[:n], refs[n:]
    x = x_ref[0]
    ms = jnp.mean(x * x, axis=-1, keepdims=True)
    xn = (x * lax.rsqrt(ms + EPS)) * g_ref[...]
    h = xn * (1.0 + scale_ref[0]) + shift_ref[0]
    hb = h.astype(BF16)
    for w_ref, o_ref, post in zip(w_refs, o_refs, posts):
        acc = _dot(hb, w_ref[...])
        if post == "sigmoid_bias":
            acc = jax.nn.sigmoid(acc + bias_ref[...])
        o_ref[0] = acc.astype(o_ref.dtype)


def _inproj(x, shift, scale, g, bias, weights, posts, out_dtypes, tm):
    bsz, seq, _ = x.shape
    mod_map = (lambda b, i: (b, 0, 0)) if shift.shape[0] > 1 else (lambda b, i: (0, 0, 0))
    in_specs = [
        pl.BlockSpec((1, tm, D_MODEL), lambda b, i: (b, i, 0)),
        pl.BlockSpec((1, 1, D_MODEL), mod_map),
        pl.BlockSpec((1, 1, D_MODEL), mod_map),
        _const_spec((1, D_MODEL)),
        _const_spec(bias.shape),
    ] + [_const_spec(w.shape) for w in weights]
    out_shape = [jax.ShapeDtypeStruct((bsz, seq, w.shape[1]), dt) for w, dt in zip(weights, out_dtypes)]
    out_specs = [pl.BlockSpec((1, tm, w.shape[1]), lambda b, i: (b, i, 0)) for w in weights]
    return pl.pallas_call(
        functools.partial(_inproj_kernel, tuple(posts)),
        out_shape=out_shape,
        grid=(bsz, seq // tm),
        in_specs=in_specs,
        out_specs=out_specs,
        compiler_params=_cparams(2),
        name="inproj",
    )(x, shift, scale, g, bias, *weights)


CONV_ROWS = 64
PACK_ROWS = 16


def _conv_silu_into(src_ref, w_ref, b_ref, dst_ref, seq):
    width = dst_ref.shape[-1]
    w0, w1, w2, bias = w_ref[0:1, :], w_ref[1:2, :], w_ref[2:3, :], b_ref[...]
    rid = lax.broadcasted_iota(jnp.int32, (CONV_ROWS, width), 0)

    def body(i, carry):
        r0 = pl.multiple_of(i * CONV_ROWS, CONV_ROWS)
        cur = src_ref[0, pl.ds(r0, CONV_ROWS), :].astype(F32)
        p0 = pl.multiple_of(jnp.maximum(r0 - PACK_ROWS, 0), PACK_ROWS)
        n0 = pl.multiple_of(jnp.minimum(r0 + CONV_ROWS, seq - PACK_ROWS), PACK_ROWS)
        prev_row = src_ref[0, pl.ds(p0, PACK_ROWS), :].astype(F32)[PACK_ROWS - 1:PACK_ROWS, :]
        next_row = src_ref[0, pl.ds(n0, PACK_ROWS), :].astype(F32)[0:1, :]
        prev_row = prev_row * (i > 0).astype(F32)
        next_row = next_row * (r0 + CONV_ROWS < seq).astype(F32)
        xm1 = jnp.where(rid == 0, prev_row, pltpu.roll(cur, 1, 0))
        xp1 = jnp.where(rid == CONV_ROWS - 1, next_row, pltpu.roll(cur, CONV_ROWS - 1, 0))
        y = w0 * xm1 + w1 * cur + w2 * xp1 + bias
        dst_ref[pl.ds(r0, CONV_ROWS), :] = _silu(y).astype(dst_ref.dtype)
        return carry

    lax.fori_loop(0, seq // CONV_ROWS, body, 0)


def _ssd_kernel(seq, seq_c,
                xs_l, bm_l, cm_l, sm_l, xs_c, bm_c, sm_c,
                cwx, cwb, cwc, cbx, cbb, cbc,
                dtb_ref, e_ref, alog_ref, dskip_ref,
                tri_ref, itile_ref, maskf_ref, maskb_ref, bd_ref,
                y_ref,
                xs_s, bm_s, cm_s, xsc_s, bmc_s,
                dtf_s, dtb_s, cumf_s, cumb_s, hf_s, hb_s, sb_s, eb_s, hrun):
    nc, nc_c = seq // CHUNK, seq_c // CHUNK
    gw = SSD_GROUP_W

    _conv_silu_into(xs_c, cwx, cbx, xsc_s, seq_c)
    _conv_silu_into(bm_c, cwb, cbb, bmc_s, seq_c)
    _conv_silu_into(xs_l, cwx, cbx, xs_s, seq)
    _conv_silu_into(bm_l, cwb, cbb, bm_s, seq)
    _conv_silu_into(cm_l, cwc, cbc, cm_s, seq)

    a_f = -jnp.exp(alog_ref[0, 0:1, :])
    a_b = -jnp.exp(alog_ref[0, 1:2, :])
    tri = tri_ref[...]

    def chunk_pre(sm_ref, xs_ref, bm_ref, r0):
        dt = _softplus(sm_ref[0, pl.ds(r0, CHUNK), :] + dtb_ref[...])
        dt_hi = dt.astype(BF16)
        dt_lo = (dt - dt_hi.astype(F32)).astype(BF16)
        dt_f = _dot(dt_hi, e_ref[0, 0]) + _dot(dt_lo, e_ref[0, 0])
        dt_b = _dot(dt_hi, e_ref[0, 1]) + _dot(dt_lo, e_ref[0, 1])
        da_f, da_b = dt_f * a_f, dt_b * a_b
        da = jnp.concatenate([da_f, da_b], axis=1)
        hi, mid, lo = _split3(da)
        pre = _dot(tri, hi) + _dot(tri, mid) + _dot(tri, lo)
        cum_f = pre[:, :gw]
        tot_b = pre[CHUNK - 1:CHUNK, gw:]
        cum_b = tot_b - pre[:, gw:] + da_b
        last_f = cum_f[CHUNK - 1:CHUNK, :]
        xs = xs_ref[pl.ds(r0, CHUNK), :]
        w_f = (xs * (dt_f * jnp.exp(last_f - cum_f))).astype(BF16)
        w_b = (xs * (dt_b * jnp.exp(tot_b - cum_b))).astype(BF16)
        st = _dot_tn(bm_ref[pl.ds(r0, CHUNK), :], jnp.concatenate([w_f, w_b], axis=1))
        return dt_f, dt_b, cum_f, cum_b, st[:, :gw], st[:, gw:], jnp.exp(last_f), jnp.exp(tot_b)

    hrun[...] = jnp.zeros_like(hrun)

    def ctx_fwd(j, carry):
        r0 = pl.multiple_of(j * CHUNK, CHUNK)
        _, _, _, _, s_f, s_b, e_f, e_b = chunk_pre(sm_c, xsc_s, bmc_s, r0)
        hrun[...] = e_f * hrun[...] + s_f
        sb_s[j] = s_b.astype(BF16)
        eb_s[j] = jnp.broadcast_to(e_b, (8, gw))
        return carry

    lax.fori_loop(0, nc_c, ctx_fwd, 0)

    def lat_fwd(j, carry):
        r0 = pl.multiple_of(j * CHUNK, CHUNK)
        dt_f, dt_b, cum_f, cum_b, s_f, s_b, e_f, e_b = chunk_pre(sm_l, xs_s, bm_s, r0)
        dtf_s[pl.ds(r0, CHUNK), :] = dt_f
        dtb_s[pl.ds(r0, CHUNK), :] = dt_b
        cumf_s[pl.ds(r0, CHUNK), :] = cum_f
        cumb_s[pl.ds(r0, CHUNK), :] = cum_b
        h = hrun[...]
        hf_s[j] = h.astype(BF16)
        hrun[...] = e_f * h + s_f
        sb_s[nc_c + j] = s_b.astype(BF16)
        eb_s[nc_c + j] = jnp.broadcast_to(e_b, (8, gw))
        return carry

    lax.fori_loop(0, nc, lat_fwd, 0)

    hrun[...] = jnp.zeros_like(hrun)

    def ctx_bwd(i, carry):
        j = nc_c - 1 - i
        hrun[...] = eb_s[j][0:1, :] * hrun[...] + sb_s[j].astype(F32)
        return carry

    lax.fori_loop(0, nc_c, ctx_bwd, 0)

    def lat_bwd(i, carry):
        j = nc - 1 - i
        h = hrun[...]
        hb_s[j] = h.astype(BF16)
        hrun[...] = eb_s[nc_c + j][0:1, :] * h + sb_s[nc_c + j].astype(F32)
        return carry

    lax.fori_loop(0, nc, lat_bwd, 0)

    itile = itile_ref[...]
    dskip = dskip_ref[0]

    def rowvec(v):
        return jnp.sum(v * itile, axis=0, keepdims=True)

    def lat_out(j, carry):
        r0 = pl.multiple_of(j * CHUNK, CHUNK)
        rows = pl.ds(r0, CHUNK)
        xs = xs_s[rows, :]
        bc = bm_s[rows, :]
        cc = cm_s[rows, :]
        cb4 = _dot_nt(cc, jnp.concatenate([bc, bc, bc, bc], axis=0))
        cum_f, cum_b = cumf_s[rows, :], cumb_s[rows, :]
        lf = jnp.exp(jnp.where(maskf_ref[...] > 0.0, cum_f - rowvec(cum_f), NEG_BIG)) * rowvec(dtf_s[rows, :])
        lb = jnp.exp(jnp.where(maskb_ref[...] > 0.0, cum_b - rowvec(cum_b), NEG_BIG)) * rowvec(dtb_s[rows, :])
        m = (cb4 * (lf + lb)).astype(BF16)
        xb = xs.astype(BF16)
        xbd = jnp.concatenate([xb, xb, xb, xb], axis=0) * bd_ref[...]
        y = _dot(m, xbd)
        y = y + jnp.exp(cum_f) * _dot(cc, hf_s[j]) + jnp.exp(cum_b) * _dot(cc, hb_s[j])
        y_ref[0, rows, :] = y + dskip * xs
        return carry

    lax.fori_loop(0, nc, lat_out, 0)


def _ssd_consts():
    q = CHUNK
    idx = np.arange(q)
    tri = (idx[:, None] >= idx[None, :]).astype(np.float32)
    itile = np.tile(np.eye(q, dtype=np.float32), (1, SSD_GROUP_W // q))
    maskf = np.tile(tri, (1, SSD_GROUP_W // q))
    maskb = np.tile(tri.T, (1, SSD_GROUP_W // q))
    blk = np.arange(SSD_GROUP_W) // SSD_HEAD_DIM
    bd = (blk[:, None] == blk[None, :]).astype(np.float32)
    e = np.zeros((SSD_GROUPS, 2, SMALL_W, SSD_GROUP_W), np.float32)
    for g in range(SSD_GROUPS):
        for d in range(2):
            for r in range(SSD_GROUPS):
                e[g, d, d * SSD_HEADS + g * 4 + r, r * SSD_HEAD_DIM:(r + 1) * SSD_HEAD_DIM] = 1.0
    return (jnp.asarray(tri, BF16), jnp.asarray(itile), jnp.asarray(maskf), jnp.asarray(maskb),
            jnp.asarray(bd, BF16), jnp.asarray(e, BF16))


def _ssd(xbc_l, sm_l, xbc_c, sm_c, conv_w, conv_b, dt_bias, a_log, d_skip):
    bsz, seq, _ = xbc_l.shape
    seq_c = xbc_c.shape[1]
    nc, nct = seq // CHUNK, (seq + seq_c) // CHUNK
    gw, ns, ng = SSD_GROUP_W, SSD_STATE, SSD_GROUPS
    tri, itile, maskf, maskb, bd, e = _ssd_consts()
    dtb = jnp.zeros((1, SMALL_W), F32).at[0, :DT_COLS].set(dt_bias.reshape(-1))
    alog = jnp.repeat(a_log.reshape(2, ng, 4), SSD_HEAD_DIM, axis=-1).transpose(1, 0, 2)
    dsk = jnp.repeat(d_skip.reshape(ng, 4), SSD_HEAD_DIM, axis=-1).reshape(ng, 1, gw)
    b_off = SSD_D_INNER // ns
    c_off = b_off + ng
    in_specs = [
        pl.BlockSpec((1, seq, gw), lambda b, g: (b, 0, g)),
        pl.BlockSpec((1, seq, ns), lambda b, g: (b, 0, b_off + g)),
        pl.BlockSpec((1, seq, ns), lambda b, g: (b, 0, c_off + g)),
        pl.BlockSpec((1, seq, SMALL_W), lambda b, g: (b, 0, 0)),
        pl.BlockSpec((1, seq_c, gw), lambda b, g: (b, 0, g)),
        pl.BlockSpec((1, seq_c, ns), lambda b, g: (b, 0, b_off + g)),
        pl.BlockSpec((1, seq_c, SMALL_W), lambda b, g: (b, 0, 0)),
        pl.BlockSpec((3, gw), lambda b, g: (0, g)),
        pl.BlockSpec((3, ns), lambda b, g: (0, b_off + g)),
        pl.BlockSpec((3, ns), lambda b, g: (0, c_off + g)),
        pl.BlockSpec((1, gw), lambda b, g: (0, g)),
        pl.BlockSpec((1, ns), lambda b, g: (0, b_off + g)),
        pl.BlockSpec((1, ns), lambda b, g: (0, c_off + g)),
        _const_spec((1, SMALL_W)),
        pl.BlockSpec((1, 2, SMALL_W, gw), lambda b, g: (g, 0, 0, 0)),
        pl.BlockSpec((1, 2, gw), lambda b, g: (g, 0, 0)),
        pl.BlockSpec((1, 1, gw), lambda b, g: (g, 0, 0)),
        _const_spec(tri.shape), _const_spec(itile.shape), _const_spec(maskf.shape),
        _const_spec(maskb.shape), _const_spec(bd.shape),
    ]
    scratch = [
        pltpu.VMEM((seq, gw), F32), pltpu.VMEM((seq, ns), BF16), pltpu.VMEM((seq, ns), BF16),
        pltpu.VMEM((seq_c, gw), F32), pltpu.VMEM((seq_c, ns), BF16),
        pltpu.VMEM((seq, gw), F32), pltpu.VMEM((seq, gw), F32),
        pltpu.VMEM((seq, gw), F32), pltpu.VMEM((seq, gw), F32),
        pltpu.VMEM((nc, ns, gw), BF16), pltpu.VMEM((nc, ns, gw), BF16),
        pltpu.VMEM((nct, ns, gw), BF16), pltpu.VMEM((nct, 8, gw), F32),
        pltpu.VMEM((ns, gw), F32),
    ]
    return pl.pallas_call(
        functools.partial(_ssd_kernel, seq, seq_c),
        out_shape=jax.ShapeDtypeStruct((bsz, seq, SSD_D_INNER), F32),
        grid=(bsz, ng),
        in_specs=in_specs,
        out_specs=pl.BlockSpec((1, seq, gw), lambda b, g: (b, 0, g)),
        scratch_shapes=scratch,
        compiler_params=_cparams(2),
        name="ssd",
    )(xbc_l, xbc_l, xbc_l, sm_l, xbc_c, xbc_c, sm_c,
      conv_w, conv_w, conv_w, conv_b, conv_b, conv_b,
      dtb, e, alog, dsk, tri, itile, maskf, maskb, bd)


def _gla_kernel(seq, seq_c,
                q_l, k_l, v_l, r_l, sm_l, k_c, v_c, sm_c,
                gw_ref, gb_ref, gn_ref, tri_ref, tril_ref, triu_ref,
                o_ref,
                cum_s, hs_s, sb_s, eb_s, hrun):
    nc, nc_c = seq // CHUNK, seq_c // CHUNK
    dk = GLA_DK
    tri = tri_ref[...]
    scale = GLA_DK ** -0.5
    mid = CHUNK // 2

    def chunk_pre(sm_ref, k_ref, v_ref, r0):
        rows = pl.ds(r0, CHUNK)
        x = jnp.dot(sm_ref[0, rows, :], gw_ref[0], preferred_element_type=F32,
                    precision=lax.Precision.HIGHEST) + gb_ref[0]
        la = (jnp.minimum(x, 0.0) - jnp.log1p(jnp.exp(-jnp.abs(x)))) * (1.0 / GLA_GATE_NORM)
        hi, md, lo = _split3(la)
        pre = _dot(tri, hi) + _dot(tri, md) + _dot(tri, lo)
        cum_f = pre[:, :dk]
        tot_b = pre[CHUNK - 1:CHUNK, dk:]
        cum_b = tot_b - pre[:, dk:] + la[:, dk:]
        last_f = cum_f[CHUNK - 1:CHUNK, :]
        k = k_ref[0, rows, :].astype(F32)
        kd = jnp.concatenate([k * jnp.exp(last_f - cum_f), k * jnp.exp(tot_b - cum_b)], axis=1)
        st = _dot_tn(v_ref[0, rows, :], kd.astype(BF16))
        return cum_f, cum_b, st[:, :dk], st[:, dk:], jnp.exp(last_f), jnp.exp(tot_b)

    hrun[...] = jnp.zeros_like(hrun)

    def ctx_fwd(j, carry):
        r0 = pl.multiple_of(j * CHUNK, CHUNK)
        _, _, s_f, s_b, e_f, e_b = chunk_pre(sm_c, k_c, v_c, r0)
        hrun[...] = e_f * hrun[...] + s_f
        sb_s[j] = s_b.astype(BF16)
        eb_s[j] = jnp.broadcast_to(e_b, (8, dk))
        return carry

    lax.fori_loop(0, nc_c, ctx_fwd, 0)

    def lat_fwd(j, carry):
        r0 = pl.multiple_of(j * CHUNK, CHUNK)
        cum_f, cum_b, s_f, s_b, e_f, e_b = chunk_pre(sm_l, k_l, v_l, r0)
        cum_s[pl.ds(r0, CHUNK), :] = jnp.concatenate([cum_f, cum_b], axis=1)
        h = hrun[...]
        hs_s[j, :, 0:dk] = h.astype(BF16)
        hrun[...] = e_f * h + s_f
        sb_s[nc_c + j] = s_b.astype(BF16)
        eb_s[nc_c + j] = jnp.broadcast_to(e_b, (8, dk))
        return carry

    lax.fori_loop(0, nc, lat_fwd, 0)

    hrun[...] = jnp.zeros_like(hrun)

    def ctx_bwd(i, carry):
        j = nc_c - 1 - i
        hrun[...] = eb_s[j][0:1, :] * hrun[...] + sb_s[j].astype(F32)
        return carry

    lax.fori_loop(0, nc_c, ctx_bwd, 0)

    def lat_bwd(i, carry):
        j = nc - 1 - i
        h = hrun[...]
        hs_s[j, :, dk:2 * dk] = h.astype(BF16)
        hrun[...] = eb_s[nc_c + j][0:1, :] * h + sb_s[nc_c + j].astype(F32)
        return carry

    lax.fori_loop(0, nc, lat_bwd, 0)

    def lat_out(j, carry):
        r0 = pl.multiple_of(j * CHUNK, CHUNK)
        rows = pl.ds(r0, CHUNK)
        cum = cum_s[rows, :]
        cum_f, cum_b = cum[:, :dk], cum[:, dk:]
        ref_f = cum_f[mid - 1:mid, :]
        ref_b = cum_b[mid:mid + 1, :]
        q = q_l[0, rows, :].astype(F32) * scale
        k = k_l[0, rows, :].astype(F32)
        v = v_l[0, rows, :]
        sc_f = _dot_nt((q * jnp.exp(cum_f - ref_f)).astype(BF16), (k * jnp.exp(ref_f - cum_f)).astype(BF16))
        sc_b = _dot_nt((q * jnp.exp(cum_b - ref_b)).astype(BF16), (k * jnp.exp(ref_b - cum_b)).astype(BF16))
        p = (sc_f * tril_ref[...] + sc_b * triu_ref[...]).astype(BF16)
        qe = jnp.concatenate([q * jnp.exp(cum_f), q * jnp.exp(cum_b)], axis=1).astype(BF16)
        o = _dot(p, v) + _dot_nt(qe, hs_s[j])
        o = o * lax.rsqrt(jnp.mean(o * o, axis=-1, keepdims=True) + EPS) * gn_ref[...]
        o_ref[0, rows, :] = (o * _silu(r_l[0, rows, :].astype(F32))).astype(o_ref.dtype)
        return carry

    lax.fori_loop(0, nc, lat_out, 0)


def _gla(q_l, k_l, v_l, r_l, sm_l, k_c, v_c, sm_c, gate_w, gate_b, norm_g):
    bsz, seq, _ = q_l.shape
    seq_c = k_c.shape[1]
    nc, nct = seq // CHUNK, (seq + seq_c) // CHUNK
    dk, dv, nh = GLA_DK, GLA_DV, GLA_HEADS
    idx = np.arange(CHUNK)
    tril = (idx[:, None] >= idx[None, :]).astype(np.float32)
    gw = jnp.zeros((nh, SMALL_W, 2 * dk), F32)
    gwh = gate_w.reshape(2, GLA_GATE_RANK, nh, dk).transpose(2, 1, 0, 3).reshape(nh, GLA_GATE_RANK, 2 * dk)
    gw = gw.at[:, DT_COLS:DT_COLS + GLA_GATE_RANK, :].set(gwh)
    gb = gate_b.reshape(2, nh, dk).transpose(1, 0, 2).reshape(nh, 1, 2 * dk)
    in_specs = [
        pl.BlockSpec((1, seq, dk), lambda b, h: (b, 0, h)),
        pl.BlockSpec((1, seq, dk), lambda b, h: (b, 0, h)),
        pl.BlockSpec((1, seq, dv), lambda b, h: (b, 0, h)),
        pl.BlockSpec((1, seq, dv), lambda b, h: (b, 0, h)),
        pl.BlockSpec((1, seq, SMALL_W), lambda b, h: (b, 0, 0)),
        pl.BlockSpec((1, seq_c, dk), lambda b, h: (b, 0, h)),
        pl.BlockSpec((1, seq_c, dv), lambda b, h: (b, 0, h)),
        pl.BlockSpec((1, seq_c, SMALL_W), lambda b, h: (b, 0, 0)),
        pl.BlockSpec((1, SMALL_W, 2 * dk), lambda b, h: (h, 0, 0)),
        pl.BlockSpec((1, 1, 2 * dk), lambda b, h: (h, 0, 0)),
        _const_spec((1, dv)),
        _const_spec((CHUNK, CHUNK)), _const_spec((CHUNK, CHUNK)), _const_spec((CHUNK, CHUNK)),
    ]
    scratch = [
        pltpu.VMEM((seq, 2 * dk), F32),
        pltpu.VMEM((nc, dv, 2 * dk), BF16),
        pltpu.VMEM((nct, dv, dk), BF16),
        pltpu.VMEM((nct, 8, dk), F32),
        pltpu.VMEM((dv, dk), F32),
    ]
    return pl.pallas_call(
        functools.partial(_gla_kernel, seq, seq_c),
        out_shape=jax.ShapeDtypeStruct((bsz, seq, nh * dv), BF16),
        grid=(bsz, nh),
        in_specs=in_specs,
        out_specs=pl.BlockSpec((1, seq, dv), lambda b, h: (b, 0, h)),
        scratch_shapes=scratch,
        compiler_params=_cparams(2),
        name="gla",
    )(q_l, k_l, v_l, r_l, sm_l, k_c, v_c, sm_c, gw, gb, norm_g.reshape(1, dv),
      jnp.asarray(tril, BF16), jnp.asarray(tril), jnp.asarray(tril.T))


def _merge_up_kernel(y_ref, z_ref, o_ref, gt_ref, x_ref, g1_ref, sh2_ref, sc2_ref,
                     sg_ref, n2_ref, wbs_ref, wbg_ref, wo_ref, wup_ref,
                     x1_ref, u_ref):
    y = y_ref[0] * _silu(z_ref[0].astype(F32))
    y = y * lax.rsqrt(jnp.mean(y * y, axis=-1, keepdims=True) + EPS) * sg_ref[...]
    ys = _dot(y.astype(BF16), wbs_ref[...])
    os_ = _dot(o_ref[0], wbg_ref[...])
    gt = gt_ref[0].astype(F32)
    m = gt[:, :D_MODEL] * ys + gt[:, D_MODEL:] * os_
    out = _dot(m.astype(BF16), wo_ref[...])
    x1 = x_ref[0] + g1_ref[0] * out
    x1_ref[0] = x1
    h2 = x1 * lax.rsqrt(jnp.mean(x1 * x1, axis=-1, keepdims=True) + EPS) * n2_ref[...]
    h2 = h2 * (1.0 + sc2_ref[0]) + sh2_ref[0]
    u_ref[0] = _dot(h2.astype(BF16), wup_ref[...]).astype(u_ref.dtype)


def _merge_up(y_raw, z, o, gates, x, gate1, shift2, scale2, ssd_g, norm2_g, w_bs, w_bg, w_o, w_up, tm):
    bsz, seq, _ = x.shape
    n_up = w_up.shape[1]
    tok = lambda w: pl.BlockSpec((1, tm, w), lambda b, i: (b, i, 0))
    mod = pl.BlockSpec((1, 1, D_MODEL), lambda b, i: (b, 0, 0))
    return pl.pallas_call(
        _merge_up_kernel,
        out_shape=[jax.ShapeDtypeStruct((bsz, seq, D_MODEL), F32),
                   jax.ShapeDtypeStruct((bsz, seq, n_up), BF16)],
        grid=(bsz, seq // tm),
        in_specs=[tok(D_MODEL), tok(D_MODEL), tok(D_MODEL), tok(2 * D_MODEL), tok(D_MODEL),
                  mod, mod, mod, _const_spec((1, D_MODEL)), _const_spec((1, D_MODEL)),
                  _const_spec(w_bs.shape), _const_spec(w_bg.shape), _const_spec(w_o.shape),
                  _const_spec(w_up.shape)],
        out_specs=[tok(D_MODEL), tok(n_up)],
        compiler_params=_cparams(2),
        name="merge_up",
    )(y_raw, z, o, gates, x, gate1, shift2, scale2, ssd_g, norm2_g, w_bs, w_bg, w_o, w_up)


FFN_TILE_ROWS = 8
FFN_CBLK = 256


def _ffn_down_kernel(n_row_tiles,
                     um_ref, ut_ref, ub_ref, x1_ref, g2_ref, cw_ref, cb_ref, wd_ref, fg_ref,
                     o_ref,
                     sh_g, sh_v, act_s):
    i = pl.program_id(1)
    tr, gw, cb = FFN_TILE_ROWS, GRID_W, FFN_CBLK
    top_on = (i > 0).astype(F32)
    bot_on = (i < n_row_tiles - 1).astype(F32)
    rid = lax.broadcasted_iota(jnp.int32, (gw, cb), 0)

    def put_shifted(dst, slab_idx, s):
        r0 = pl.multiple_of(slab_idx * gw, gw)
        dst[0, pl.ds(r0, gw), :] = jnp.where(rid == 0, 0.0, pltpu.roll(s, 1, 0))
        dst[1, pl.ds(r0, gw), :] = s
        dst[2, pl.ds(r0, gw), :] = jnp.where(rid == gw - 1, 0.0, pltpu.roll(s, gw - 1, 0))

    for jb in range(D_FF // cb):
        for dst, c0 in ((sh_g, jb * cb), (sh_v, D_FF + jb * cb)):
            put_shifted(dst, 0, ut_ref[0, :, c0:c0 + cb].astype(F32) * top_on)
            put_shifted(dst, tr + 1, ub_ref[0, :, c0:c0 + cb].astype(F32) * bot_on)

            def fill(s, carry, dst=dst, c0=c0):
                r0 = pl.multiple_of(s * gw, gw)
                put_shifted(dst, s + 1, um_ref[0, pl.ds(r0, gw), c0:c0 + cb].astype(F32))
                return carry

            lax.fori_loop(0, tr, fill, 0)

        def conv_row(rr, carry, jb=jb):
            def conv(dst, c0):
                acc = jnp.broadcast_to(cb_ref[0:1, c0:c0 + cb], (gw, cb))
                for dr in range(3):
                    r0 = pl.multiple_of((rr + dr) * gw, gw)
                    for dc in range(3):
                        t = dr * 3 + dc
                        acc = acc + cw_ref[t:t + 1, c0:c0 + cb] * dst[dc, pl.ds(r0, gw), :]
                return acc

            gate = conv(sh_g, jb * cb)
            val = conv(sh_v, D_FF + jb * cb)
            o0 = pl.multiple_of(rr * gw, gw)
            act_s[pl.ds(o0, gw), jb * cb:(jb + 1) * cb] = (_silu(gate) * val).astype(BF16)
            return carry

        lax.fori_loop(0, tr, conv_row, 0)

    ffn = _dot(act_s[...], wd_ref[...])
    x2 = x1_ref[0] + g2_ref[0] * ffn
    o_ref[0] = x2 * lax.rsqrt(jnp.mean(x2 * x2, axis=-1, keepdims=True) + EPS) * fg_ref[...]


def _ffn_down(u, x1, gate2, conv_w, conv_b, w_down, final_g):
    bsz, seq, n_up = u.shape
    n_rows = seq // GRID_W
    tr = FFN_TILE_ROWS
    n_tiles = n_rows // tr
    tm = tr * GRID_W
    return pl.pallas_call(
        functools.partial(_ffn_down_kernel, n_tiles),
        out_shape=jax.ShapeDtypeStruct((bsz, seq, D_MODEL), F32),
        grid=(bsz, n_tiles),
        in_specs=[
            pl.BlockSpec((1, tm, n_up), lambda b, i: (b, i, 0)),
            pl.BlockSpec((1, GRID_W, n_up), lambda b, i: (b, jnp.maximum(i * tr - 1, 0), 0)),
            pl.BlockSpec((1, GRID_W, n_up), lambda b, i: (b, jnp.minimum((i + 1) * tr, n_rows - 1), 0)),
            pl.BlockSpec((1, tm, D_MODEL), lambda b, i: (b, i, 0)),
            pl.BlockSpec((1, 1, D_MODEL), lambda b, i: (b, 0, 0)),
            _const_spec(conv_w.shape), _const_spec(conv_b.shape),
            _const_spec(w_down.shape), _const_spec((1, D_MODEL)),
        ],
        out_specs=pl.BlockSpec((1, tm, D_MODEL), lambda b, i: (b, i, 0)),
        scratch_shapes=[
            pltpu.VMEM((3, (tr + 2) * GRID_W, FFN_CBLK), F32),
            pltpu.VMEM((3, (tr + 2) * GRID_W, FFN_CBLK), F32),
            pltpu.VMEM((tm, D_FF), BF16),
        ],
        compiler_params=_cparams(2),
        name="ffn_down",
    )(u, u, u, x1, gate2, conv_w, conv_b, w_down, final_g)


def kernel(x, c, ctx, c_ctx, w_ada, b_ada, norm1_g, w_in, ssd_conv_w, ssd_conv_b, ssd_dt_bias,
           ssd_a_log, ssd_d, ssd_norm_g, gla_gate_w, gla_gate_b, gla_norm_g, w_br_ssd, w_br_gla,
           w_merge, b_merge, w_o, norm2_g, w_up, ffn_conv_w, ffn_conv_b, w_down, final_norm_g):
    bsz, seq, _ = x.shape
    assert w_ada.shape[0] == 1, "single-layer trunk"
    row = lambda v: v.reshape(1, -1)

    n_cond = bsz + 1
    pad = (-n_cond) % 8
    cond = jnp.concatenate([c, c_ctx[None], jnp.zeros((pad, D_MODEL), F32)], axis=0)
    mod = _adaln(cond, w_ada[0], row(b_ada[0]))
    mx = mod[:bsz].reshape(bsz, N_MOD, 1, D_MODEL)
    mc = mod[bsz:bsz + 1].reshape(1, N_MOD, 1, D_MODEL)

    wi = w_in[0]
    o_z, o_xbc = 0, SSD_D_INNER
    o_dt = o_xbc + SSD_D_INNER + 2 * SSD_GROUPS * SSD_STATE
    o_q = o_dt + DT_COLS
    o_k = o_q + GLA_KEY_DIM
    o_v = o_k + GLA_KEY_DIM
    o_g = o_v + D_MODEL
    o_r = o_g + GLA_GATE_RANK
    w_small = jnp.zeros((D_MODEL, SMALL_W), F32)
    w_small = w_small.at[:, :DT_COLS].set(wi[:, o_dt:o_q])
    w_small = w_small.at[:, DT_COLS:DT_COLS + GLA_GATE_RANK].set(wi[:, o_g:o_r])
    bf = lambda w: w.astype(BF16)
    w_z, w_xbc = bf(wi[:, o_z:o_xbc]), bf(wi[:, o_xbc:o_dt])
    w_q, w_k, w_v, w_r = bf(wi[:, o_q:o_k]), bf(wi[:, o_k:o_v]), bf(wi[:, o_v:o_g]), bf(wi[:, o_r:])
    w_small = bf(w_small)
    bm = row(b_merge[0])
    g1 = row(norm1_g[0])

    xbc_c, k_c, v_c, sm_c = _inproj(
        ctx, mc[:, 0], mc[:, 1], g1, bm, [w_xbc, w_k, w_v, w_small],
        ["none"] * 4, [BF16, BF16, BF16, F32], tm=ctx.shape[1])
    z, xbc, q, k, v, r, gates, sm = _inproj(
        x, mx[:, 0], mx[:, 1], g1, bm,
        [w_z, w_xbc, w_q, w_k, w_v, w_r, bf(w_merge[0]), w_small],
        ["none"] * 6 + ["sigmoid_bias", "none"],
        [BF16] * 7 + [F32], tm=512)

    y_raw = _ssd(xbc, sm, xbc_c, sm_c, ssd_conv_w[0], row(ssd_conv_b[0]),
                 ssd_dt_bias[0], ssd_a_log[0], ssd_d[0])
    o = _gla(q, k, v, r, sm, k_c, v_c, sm_c, gla_gate_w[0], gla_gate_b[0], gla_norm_g[0])

    x1, u = _merge_up(y_raw, z, o, gates, x, mx[:, 2], mx[:, 3], mx[:, 4],
                      row(ssd_norm_g[0]), row(norm2_g[0]),
                      bf(w_br_ssd[0]), bf(w_br_gla[0]), bf(w_o[0]), bf(w_up[0]), tm=512)
    return _ffn_down(u, x1, mx[:, 5], ffn_conv_w[0].reshape(9, -1), row(ffn_conv_b[0]),
                     bf(w_down[0]), row(final_norm_g))
```

```python
import functools
import math

import numpy as np
import jax
import jax.numpy as jnp
from jax import lax
from jax.experimental import pallas as pl
from jax.experimental.pallas import tpu as pltpu

F32 = jnp.float32
BF16 = jnp.bfloat16

D_MODEL = 1024
GRID_W = 64
CHUNK = 64
EPS = 1e-6
N_MOD = 6

SSD_HEAD_DIM = 64
SSD_HEADS = 16
SSD_GROUPS = 4
SSD_STATE = 128
SSD_GROUP_W = (SSD_HEADS // SSD_GROUPS) * SSD_HEAD_DIM
SSD_D_INNER = D_MODEL

GLA_HEADS = 4
GLA_DK = 128
GLA_DV = 256
GLA_GATE_RANK = 16
GLA_GATE_NORM = 16.0
GLA_KEY_DIM = GLA_HEADS * GLA_DK

D_FF = 2816
SMALL_W = 128
DT_COLS = 2 * SSD_HEADS
NEG_BIG = -1e30
MXU_DIM = 256
SCAN_UNROLL = 4
BLOCK_UNROLL = 2

VMEM_LIMIT_BYTES = 56 * 1024 * 1024


def _cparams(n_grid):
    return pltpu.CompilerParams(
        dimension_semantics=("arbitrary",) * n_grid,
        vmem_limit_bytes=VMEM_LIMIT_BYTES,
    )


def _const_spec(shape):
    nd = len(shape)
    return pl.BlockSpec(tuple(shape), lambda *_: (0,) * nd, pipeline_mode=pl.Buffered(1))


def _silu(v):
    return v * jax.nn.sigmoid(v)


def _softplus(v):
    return jnp.maximum(v, 0.0) + jnp.log1p(jnp.exp(-jnp.abs(v)))


def _split2(v):
    hi = v.astype(BF16)
    lo = (v - hi.astype(F32)).astype(BF16)
    return hi, lo


def _dot(a, b):
    return jnp.dot(a, b, preferred_element_type=F32)


def _dot_tn(a, b):
    return lax.dot_general(a, b, (((0,), (0,)), ((), ())), preferred_element_type=F32)


def _dot_nt(a, b):
    return lax.dot_general(a, b, (((1,), (1,)), ((), ())), preferred_element_type=F32)


def _scan_block_rows(seq, seq_c):
    return math.gcd(math.gcd(seq, seq_c), MXU_DIM)


def _block_tri(rows):
    idx = np.arange(rows)
    same = (idx[:, None] // CHUNK) == (idx[None, :] // CHUNK)
    return jnp.asarray((same & (idx[:, None] >= idx[None, :])).astype(np.float32), BF16)


def _adaln_kernel(c_ref, w_ref, b_ref, o_ref):
    s = _silu(c_ref[...])
    o_ref[...] = jnp.dot(s, w_ref[...], preferred_element_type=F32,
                         precision=lax.Precision.HIGHEST) + b_ref[...]


def _adaln(cond, w, b):
    rows = cond.shape[0]
    n_out = w.shape[1]
    tn = D_MODEL
    return pl.pallas_call(
        _adaln_kernel,
        out_shape=jax.ShapeDtypeStruct((rows, n_out), F32),
        grid=(n_out // tn,),
        in_specs=[
            pl.BlockSpec((rows, D_MODEL), lambda j: (0, 0)),
            pl.BlockSpec((D_MODEL, tn), lambda j: (0, j)),
            pl.BlockSpec((1, tn), lambda j: (0, j)),
        ],
        out_specs=pl.BlockSpec((rows, tn), lambda j: (0, j)),
        compiler_params=_cparams(1),
        name="adaln",
    )(cond, w, b)


def _inproj_kernel(posts, x_ref, shift_ref, scale_ref, g_ref, bias_ref, *refs):
    n = len(posts)
    w_refs, o_refs = refs[:n], refs[n:]
    x = x_ref[0]
    ms = jnp.mean(x * x, axis=-1, keepdims=True)
    xn = (x * lax.rsqrt(ms + EPS)) * g_ref[...]
    h = xn * (1.0 + scale_ref[0]) + shift_ref[0]
    hb = h.astype(BF16)
    for w_ref, o_ref, post in zip(w_refs, o_refs, posts):
        acc = _dot(hb, w_ref[...])
        if post == "sigmoid_bias":
            acc = jax.nn.sigmoid(acc + bias_ref[...])
        o_ref[0] = acc.astype(o_ref.dtype)


def _inproj(x, shift, scale, g, bias, weights, posts, out_dtypes, tm):
    bsz, seq, _ = x.shape
    mod_map = (lambda b, i: (b, 0, 0)) if shift.shape[0] > 1 else (lambda b, i: (0, 0, 0))
    in_specs = [
        pl.BlockSpec((1, tm, D_MODEL), lambda b, i: (b, i, 0)),
        pl.BlockSpec((1, 1, D_MODEL), mod_map),
        pl.BlockSpec((1, 1, D_MODEL), mod_map),
        _const_spec((1, D_MODEL)),
        _const_spec(bias.shape),
    ] + [_const_spec(w.shape) for w in weights]
    out_shape = [jax.ShapeDtypeStruct((bsz, seq, w.shape[1]), dt) for w, dt in zip(weights, out_dtypes)]
    out_specs = [pl.BlockSpec((1, tm, w.shape[1]), lambda b, i: (b, i, 0)) for w in weights]
    return pl.pallas_call(
        functools.partial(_inproj_kernel, tuple(posts)),
        out_shape=out_shape,
        grid=(bsz, seq // tm),
        in_specs=in_specs,
        out_specs=out_specs,
        compiler_params=_cparams(2),
        name="inproj",
    )(x, shift, scale, g, bias, *weights)


CONV_ROWS = 64
PACK_ROWS = 16


def _conv_silu_into(src_ref, w_ref, b_ref, dst_ref, dst_off, seq):
    width = dst_ref.shape[-1]
    w0, w1, w2, bias = w_ref[0:1, :], w_ref[1:2, :], w_ref[2:3, :], b_ref[...]
    rid = lax.broadcasted_iota(jnp.int32, (CONV_ROWS, width), 0)

    def body(i, carry):
        r0 = pl.multiple_of(i * CONV_ROWS, CONV_ROWS)
        cur = src_ref[0, pl.ds(r0, CONV_ROWS), :].astype(F32)
        p0 = pl.multiple_of(jnp.maximum(r0 - PACK_ROWS, 0), PACK_ROWS)
        n0 = pl.multiple_of(jnp.minimum(r0 + CONV_ROWS, seq - PACK_ROWS), PACK_ROWS)
        prev_row = src_ref[0, pl.ds(p0, PACK_ROWS), :].astype(F32)[PACK_ROWS - 1:PACK_ROWS, :]
        next_row = src_ref[0, pl.ds(n0, PACK_ROWS), :].astype(F32)[0:1, :]
        prev_row = prev_row * (i > 0).astype(F32)
        next_row = next_row * (r0 + CONV_ROWS < seq).astype(F32)
        xm1 = jnp.where(rid == 0, prev_row, pltpu.roll(cur, 1, 0))
        xp1 = jnp.where(rid == CONV_ROWS - 1, next_row, pltpu.roll(cur, CONV_ROWS - 1, 0))
        y = w0 * xm1 + w1 * cur + w2 * xp1 + bias
        d0 = pl.multiple_of(dst_off + r0, CONV_ROWS)
        dst_ref[pl.ds(d0, CONV_ROWS), :] = _silu(y).astype(dst_ref.dtype)
        return carry

    lax.fori_loop(0, seq // CONV_ROWS, body, 0, unroll=BLOCK_UNROLL)


def _bwd_chunk_order(i, nc_c, nct):
    return jnp.where(i < nc_c, nc_c - 1 - i, nct - 1 - (i - nc_c))


def _ssd_kernel(seq, seq_c, blk,
                xs_l, bm_l, cm_l, sm_l, xs_c, bm_c, sm_c,
                cwx, cwb, cwc, cbx, cbb, cbc,
                dtb_ref, e_ref, alog_ref, dskip_ref,
                tri_ref, itile_ref, maskf_ref, maskb_ref, bd_ref,
                y_ref,
                xs_s, bm_s, cm_s, dt_s, dtf_s, dtb_s, cumf_s, cumb_s,
                sf_s, sb_s, ef_s, eb_s, hf_s, hb_s, hrun_f, hrun_b):
    nc, nc_c = seq // CHUNK, seq_c // CHUNK
    nct = nc + nc_c
    seq_t = seq + seq_c
    gw = SSD_GROUP_W
    cpb = blk // CHUNK

    _conv_silu_into(xs_c, cwx, cbx, xs_s, 0, seq_c)
    _conv_silu_into(bm_c, cwb, cbb, bm_s, 0, seq_c)
    _conv_silu_into(xs_l, cwx, cbx, xs_s, seq_c, seq)
    _conv_silu_into(bm_l, cwb, cbb, bm_s, seq_c, seq)
    _conv_silu_into(cm_l, cwc, cbc, cm_s, 0, seq)

    def dt_rows(src_ref, dst_off, n_rows):
        def body(i, carry):
            r0 = pl.multiple_of(i * blk, blk)
            d0 = pl.multiple_of(dst_off + r0, CHUNK)
            dt_s[pl.ds(d0, blk), :] = _softplus(src_ref[0, pl.ds(r0, blk), :] + dtb_ref[...])
            return carry
        lax.fori_loop(0, n_rows // blk, body, 0, unroll=BLOCK_UNROLL)

    dt_rows(sm_c, 0, seq_c)
    dt_rows(sm_l, seq_c, seq)

    def expand(i, carry):
        rows = pl.ds(pl.multiple_of(i * blk, blk), blk)
        hi, lo = _split2(dt_s[rows, :])
        dtf_s[rows, :] = _dot(hi, e_ref[0, 0]) + _dot(lo, e_ref[0, 0])
        dtb_s[rows, :] = _dot(hi, e_ref[0, 1]) + _dot(lo, e_ref[0, 1])
        return carry

    lax.fori_loop(0, seq_t // blk, expand, 0, unroll=BLOCK_UNROLL)

    a_f = -jnp.exp(alog_ref[0, 0:1, :])
    a_b = -jnp.exp(alog_ref[0, 1:2, :])
    tri = tri_ref[...]

    def cumsums(i, carry):
        r0 = pl.multiple_of(i * blk, blk)
        rows = pl.ds(r0, blk)
        da_b = dtb_s[rows, :] * a_b
        hi, lo = _split2(jnp.concatenate([dtf_s[rows, :] * a_f, da_b], axis=1))
        pre = _dot(tri, hi) + _dot(tri, lo)
        cumf_s[rows, :] = pre[:, :gw]
        for c in range(cpb):
            lo_r, hi_r = c * CHUNK, (c + 1) * CHUNK
            tot = pre[hi_r - 1:hi_r, gw:]
            cumb_s[pl.ds(r0 + lo_r, CHUNK), :] = tot - pre[lo_r:hi_r, gw:] + da_b[lo_r:hi_r, :]
        return carry

    lax.fori_loop(0, seq_t // blk, cumsums, 0, unroll=BLOCK_UNROLL)

    def states(j, carry):
        rows = pl.ds(pl.multiple_of(j * CHUNK, CHUNK), CHUNK)
        cum_f, cum_b = cumf_s[rows, :], cumb_s[rows, :]
        last_f = cum_f[CHUNK - 1:CHUNK, :]
        tot_b = cum_b[0:1, :]
        xs = xs_s[rows, :]
        w_f = (xs * (dtf_s[rows, :] * jnp.exp(last_f - cum_f))).astype(BF16)
        w_b = (xs * (dtb_s[rows, :] * jnp.exp(tot_b - cum_b))).astype(BF16)
        st = _dot_tn(bm_s[rows, :], jnp.concatenate([w_f, w_b], axis=1))
        sf_s[j] = st[:, :gw].astype(BF16)
        sb_s[j] = st[:, gw:].astype(BF16)
        ef_s[j] = jnp.broadcast_to(jnp.exp(last_f), (8, gw))
        eb_s[j] = jnp.broadcast_to(jnp.exp(tot_b), (8, gw))
        return carry

    lax.fori_loop(0, nct, states, 0, unroll=SCAN_UNROLL)

    hrun_f[...] = jnp.zeros_like(hrun_f)
    hrun_b[...] = jnp.zeros_like(hrun_b)

    def recur(i, carry):
        h = hrun_f[...]
        hf_s[i] = h.astype(BF16)
        hrun_f[...] = ef_s[i][0:1, :] * h + sf_s[i].astype(F32)
        jb = _bwd_chunk_order(i, nc_c, nct)
        g = hrun_b[...]
        hb_s[jb] = g.astype(BF16)
        hrun_b[...] = eb_s[jb][0:1, :] * g + sb_s[jb].astype(F32)
        return carry

    lax.fori_loop(0, nct, recur, 0, unroll=BLOCK_UNROLL)

    itile = itile_ref[...]
    dskip = dskip_ref[0]

    def rowvec(v):
        return jnp.sum(v * itile, axis=0, keepdims=True)

    def lat_out(j, carry):
        r0 = pl.multiple_of(j * CHUNK, CHUNK)
        rows = pl.ds(r0, CHUNK)
        trow = pl.ds(pl.multiple_of(seq_c + r0, CHUNK), CHUNK)
        xs = xs_s[trow, :]
        bc = bm_s[trow, :]
        cc = cm_s[rows, :]
        cb4 = _dot_nt(cc, jnp.concatenate([bc, bc, bc, bc], axis=0))
        cum_f, cum_b = cumf_s[trow, :], cumb_s[trow, :]
        lf = jnp.exp(jnp.where(maskf_ref[...] > 0.0, cum_f - rowvec(cum_f), NEG_BIG)) * rowvec(dtf_s[trow, :])
        lb = jnp.exp(jnp.where(maskb_ref[...] > 0.0, cum_b - rowvec(cum_b), NEG_BIG)) * rowvec(dtb_s[trow, :])
        m = (cb4 * (lf + lb)).astype(BF16)
        xb = xs.astype(BF16)
        xbd = jnp.concatenate([xb, xb, xb, xb], axis=0) * bd_ref[...]
        y = _dot(m, xbd)
        y = y + jnp.exp(cum_f) * _dot(cc, hf_s[nc_c + j]) + jnp.exp(cum_b) * _dot(cc, hb_s[nc_c + j])
        y_ref[0, rows, :] = y + dskip * xs
        return carry

    lax.fori_loop(0, nc, lat_out, 0, unroll=SCAN_UNROLL)


def _ssd_consts():
    q = CHUNK
    idx = np.arange(q)
    tri = (idx[:, None] >= idx[None, :]).astype(np.float32)
    itile = np.tile(np.eye(q, dtype=np.float32), (1, SSD_GROUP_W // q))
    maskf = np.tile(tri, (1, SSD_GROUP_W // q))
    maskb = np.tile(tri.T, (1, SSD_GROUP_W // q))
    blk = np.arange(SSD_GROUP_W) // SSD_HEAD_DIM
    bd = (blk[:, None] == blk[None, :]).astype(np.float32)
    e = np.zeros((SSD_GROUPS, 2, SMALL_W, SSD_GROUP_W), np.float32)
    for g in range(SSD_GROUPS):
        for d in range(2):
            for r in range(SSD_GROUPS):
                e[g, d, d * SSD_HEADS + g * 4 + r, r * SSD_HEAD_DIM:(r + 1) * SSD_HEAD_DIM] = 1.0
    return (jnp.asarray(itile), jnp.asarray(maskf), jnp.asarray(maskb),
            jnp.asarray(bd, BF16), jnp.asarray(e, BF16))


def _ssd(xbc_l, sm_l, xbc_c, sm_c, conv_w, conv_b, dt_bias, a_log, d_skip):
    bsz, seq, _ = xbc_l.shape
    seq_c = xbc_c.shape[1]
    seq_t = seq + seq_c
    nct = seq_t // CHUNK
    blk = _scan_block_rows(seq, seq_c)
    gw, ns, ng = SSD_GROUP_W, SSD_STATE, SSD_GROUPS
    itile, maskf, maskb, bd, e = _ssd_consts()
    tri = _block_tri(blk)
    dtb = jnp.zeros((1, SMALL_W), F32).at[0, :DT_COLS].set(dt_bias.reshape(-1))
    alog = jnp.repeat(a_log.reshape(2, ng, 4), SSD_HEAD_DIM, axis=-1).transpose(1, 0, 2)
    dsk = jnp.repeat(d_skip.reshape(ng, 4), SSD_HEAD_DIM, axis=-1).reshape(ng, 1, gw)
    b_off = SSD_D_INNER // ns
    c_off = b_off + ng
    in_specs = [
        pl.BlockSpec((1, seq, gw), lambda b, g: (b, 0, g)),
        pl.BlockSpec((1, seq, ns), lambda b, g: (b, 0, b_off + g)),
        pl.BlockSpec((1, seq, ns), lambda b, g: (b, 0, c_off + g)),
        pl.BlockSpec((1, seq, SMALL_W), lambda b, g: (b, 0, 0)),
        pl.BlockSpec((1, seq_c, gw), lambda b, g: (b, 0, g)),
        pl.BlockSpec((1, seq_c, ns), lambda b, g: (b, 0, b_off + g)),
        pl.BlockSpec((1, seq_c, SMALL_W), lambda b, g: (b, 0, 0)),
        pl.BlockSpec((3, gw), lambda b, g: (0, g)),
        pl.BlockSpec((3, ns), lambda b, g: (0, b_off + g)),
        pl.BlockSpec((3, ns), lambda b, g: (0, c_off + g)),
        pl.BlockSpec((1, gw), lambda b, g: (0, g)),
        pl.BlockSpec((1, ns), lambda b, g: (0, b_off + g)),
        pl.BlockSpec((1, ns), lambda b, g: (0, c_off + g)),
        _const_spec((1, SMALL_W)),
        pl.BlockSpec((1, 2, SMALL_W, gw), lambda b, g: (g, 0, 0, 0)),
        pl.BlockSpec((1, 2, gw), lambda b, g: (g, 0, 0)),
        pl.BlockSpec((1, 1, gw), lambda b, g: (g, 0, 0)),
        _const_spec(tri.shape), _const_spec(itile.shape), _const_spec(maskf.shape),
        _const_spec(maskb.shape), _const_spec(bd.shape),
    ]
    scratch = [
        pltpu.VMEM((seq_t, gw), F32),
        pltpu.VMEM((seq_t, ns), BF16),
        pltpu.VMEM((seq, ns), BF16),
        pltpu.VMEM((seq_t, SMALL_W), F32),
        pltpu.VMEM((seq_t, gw), F32),
        pltpu.VMEM((seq_t, gw), F32),
        pltpu.VMEM((seq_t, gw), F32),
        pltpu.VMEM((seq_t, gw), F32),
        pltpu.VMEM((nct, ns, gw), BF16),
        pltpu.VMEM((nct, ns, gw), BF16),
        pltpu.VMEM((nct, 8, gw), F32),
        pltpu.VMEM((nct, 8, gw), F32),
        pltpu.VMEM((nct, ns, gw), BF16),
        pltpu.VMEM((nct, ns, gw), BF16),
        pltpu.VMEM((ns, gw), F32),
        pltpu.VMEM((ns, gw), F32),
    ]
    return pl.pallas_call(
        functools.partial(_ssd_kernel, seq, seq_c, blk),
        out_shape=jax.ShapeDtypeStruct((bsz, seq, SSD_D_INNER), F32),
        grid=(bsz, ng),
        in_specs=in_specs,
        out_specs=pl.BlockSpec((1, seq, gw), lambda b, g: (b, 0, g)),
        scratch_shapes=scratch,
        compiler_params=_cparams(2),
        name="ssd",
    )(xbc_l, xbc_l, xbc_l, sm_l, xbc_c, xbc_c, sm_c,
      conv_w, conv_w, conv_w, conv_b, conv_b, conv_b,
      dtb, e, alog, dsk, tri, itile, maskf, maskb, bd)


def _gla_kernel(seq, seq_c, blk,
                q_l, k_l, v_l, r_l, sm_l, k_c, v_c, sm_c,
                gwh_ref, gwl_ref, gb_ref, gn_ref, tri_ref, tril_ref, triu_ref,
                o_ref,
                cum_s, st_s, e_s, hs_s, hrun):
    nc, nc_c = seq // CHUNK, seq_c // CHUNK
    nct = nc + nc_c
    dk = GLA_DK
    cpb = blk // CHUNK
    tri = tri_ref[...]
    scale = GLA_DK ** -0.5
    mid = CHUNK // 2

    def gate_rows(sm_ref, dst_off, n_rows):
        def body(i, carry):
            r0 = pl.multiple_of(i * blk, blk)
            hi, lo = _split2(sm_ref[0, pl.ds(r0, blk), :])
            x = _dot(hi, gwh_ref[0]) + _dot(lo, gwh_ref[0]) + _dot(hi, gwl_ref[0]) + gb_ref[0]
            la = (jnp.minimum(x, 0.0) - jnp.log1p(jnp.exp(-jnp.abs(x)))) * (1.0 / GLA_GATE_NORM)
            lhi, llo = _split2(la)
            pre = _dot(tri, lhi) + _dot(tri, llo)
            d0 = pl.multiple_of(dst_off + r0, CHUNK)
            cum_s[pl.ds(d0, blk), 0:dk] = pre[:, :dk]
            for c in range(cpb):
                lo_r, hi_r = c * CHUNK, (c + 1) * CHUNK
                tot = pre[hi_r - 1:hi_r, dk:]
                cum_s[pl.ds(d0 + lo_r, CHUNK), dk:2 * dk] = tot - pre[lo_r:hi_r, dk:] + la[lo_r:hi_r, dk:]
            return carry
        lax.fori_loop(0, n_rows // blk, body, 0, unroll=BLOCK_UNROLL)

    gate_rows(sm_c, 0, seq_c)
    gate_rows(sm_l, seq_c, seq)

    def state_rows(k_ref, v_ref, chunk_off, n_chunks):
        def body(j, carry):
            rows = pl.ds(pl.multiple_of(j * CHUNK, CHUNK), CHUNK)
            trow = pl.ds(pl.multiple_of((chunk_off + j) * CHUNK, CHUNK), CHUNK)
            cum = cum_s[trow, :]
            cum_f, cum_b = cum[:, :dk], cum[:, dk:]
            last_f = cum_f[CHUNK - 1:CHUNK, :]
            tot_b = cum_b[0:1, :]
            k = k_ref[0, rows, :].astype(F32)
            kd = jnp.concatenate([k * jnp.exp(last_f - cum_f), k * jnp.exp(tot_b - cum_b)], axis=1)
            st_s[chunk_off + j] = _dot_tn(v_ref[0, rows, :], kd.astype(BF16)).astype(BF16)
            e_s[chunk_off + j] = jnp.broadcast_to(
                jnp.concatenate([jnp.exp(last_f), jnp.exp(tot_b)], axis=1), (8, 2 * dk))
            return carry
        lax.fori_loop(0, n_chunks, body, 0, unroll=SCAN_UNROLL)

    state_rows(k_c, v_c, 0, nc_c)
    state_rows(k_l, v_l, nc_c, nc)

    hrun[...] = jnp.zeros_like(hrun)

    def recur(i, carry):
        jb = _bwd_chunk_order(i, nc_c, nct)
        h = hrun[...]
        hs_s[i, :, 0:dk] = h[:, :dk].astype(BF16)
        hs_s[jb, :, dk:2 * dk] = h[:, dk:].astype(BF16)
        dec = jnp.concatenate([e_s[i][0:1, :dk], e_s[jb][0:1, dk:]], axis=1)
        inc = jnp.concatenate([st_s[i, :, 0:dk], st_s[jb, :, dk:2 * dk]], axis=1).astype(F32)
        hrun[...] = dec * h + inc
        return carry

    lax.fori_loop(0, nct, recur, 0, unroll=BLOCK_UNROLL)

    def lat_out(j, carry):
        r0 = pl.multiple_of(j * CHUNK, CHUNK)
        rows = pl.ds(r0, CHUNK)
        trow = pl.ds(pl.multiple_of(seq_c + r0, CHUNK), CHUNK)
        cum = cum_s[trow, :]
        cum_f, cum_b = cum[:, :dk], cum[:, dk:]
        ref_f = cum_f[mid - 1:mid, :]
        ref_b = cum_b[mid:mid + 1, :]
        q = q_l[0, rows, :].astype(F32) * scale
        k = k_l[0, rows, :].astype(F32)
        v = v_l[0, rows, :]
        sc_f = _dot_nt((q * jnp.exp(cum_f - ref_f)).astype(BF16), (k * jnp.exp(ref_f - cum_f)).astype(BF16))
        sc_b = _dot_nt((q * jnp.exp(cum_b - ref_b)).astype(BF16), (k * jnp.exp(ref_b - cum_b)).astype(BF16))
        p = (sc_f * tril_ref[...] + sc_b * triu_ref[...]).astype(BF16)
        qe = jnp.concatenate([q * jnp.exp(cum_f), q * jnp.exp(cum_b)], axis=1).astype(BF16)
        o = _dot(p, v) + _dot_nt(qe, hs_s[nc_c + j])
        o = o * lax.rsqrt(jnp.mean(o * o, axis=-1, keepdims=True) + EPS) * gn_ref[...]
        o_ref[0, rows, :] = (o * _silu(r_l[0, rows, :].astype(F32))).astype(o_ref.dtype)
        return carry

    lax.fori_loop(0, nc, lat_out, 0, unroll=SCAN_UNROLL)


def _gla(q_l, k_l, v_l, r_l, sm_l, k_c, v_c, sm_c, gate_w, gate_b, norm_g):
    bsz, seq, _ = q_l.shape
    seq_c = k_c.shape[1]
    seq_t = seq + seq_c
    nct = seq_t // CHUNK
    blk = _scan_block_rows(seq, seq_c)
    dk, dv, nh = GLA_DK, GLA_DV, GLA_HEADS
    idx = np.arange(CHUNK)
    tril = (idx[:, None] >= idx[None, :]).astype(np.float32)
    gw = jnp.zeros((nh, SMALL_W, 2 * dk), F32)
    gwh = gate_w.reshape(2, GLA_GATE_RANK, nh, dk).transpose(2, 1, 0, 3).reshape(nh, GLA_GATE_RANK, 2 * dk)
    gw = gw.at[:, DT_COLS:DT_COLS + GLA_GATE_RANK, :].set(gwh)
    gw_hi = gw.astype(BF16)
    gw_lo = (gw - gw_hi.astype(F32)).astype(BF16)
    gb = gate_b.reshape(2, nh, dk).transpose(1, 0, 2).reshape(nh, 1, 2 * dk)
    in_specs = [
        pl.BlockSpec((1, seq, dk), lambda b, h: (b, 0, h)),
        pl.BlockSpec((1, seq, dk), lambda b, h: (b, 0, h)),
        pl.BlockSpec((1, seq, dv), lambda b, h: (b, 0, h)),
        pl.BlockSpec((1, seq, dv), lambda b, h: (b, 0, h)),
        pl.BlockSpec((1, seq, SMALL_W), lambda b, h: (b, 0, 0)),
        pl.BlockSpec((1, seq_c, dk), lambda b, h: (b, 0, h)),
        pl.BlockSpec((1, seq_c, dv), lambda b, h: (b, 0, h)),
        pl.BlockSpec((1, seq_c, SMALL_W), lambda b, h: (b, 0, 0)),
        pl.BlockSpec((1, SMALL_W, 2 * dk), lambda b, h: (h, 0, 0)),
        pl.BlockSpec((1, SMALL_W, 2 * dk), lambda b, h: (h, 0, 0)),
        pl.BlockSpec((1, 1, 2 * dk), lambda b, h: (h, 0, 0)),
        _const_spec((1, dv)),
        _const_spec((blk, blk)), _const_spec((CHUNK, CHUNK)), _const_spec((CHUNK, CHUNK)),
    ]
    scratch = [
        pltpu.VMEM((seq_t, 2 * dk), F32),
        pltpu.VMEM((nct, dv, 2 * dk), BF16),
        pltpu.VMEM((nct, 8, 2 * dk), F32),
        pltpu.VMEM((nct, dv, 2 * dk), BF16),
        pltpu.VMEM((dv, 2 * dk), F32),
    ]
    return pl.pallas_call(
        functools.partial(_gla_kernel, seq, seq_c, blk),
        out_shape=jax.ShapeDtypeStruct((bsz, seq, nh * dv), BF16),
        grid=(bsz, nh),
        in_specs=in_specs,
        out_specs=pl.BlockSpec((1, seq, dv), lambda b, h: (b, 0, h)),
        scratch_shapes=scratch,
        compiler_params=_cparams(2),
        name="gla",
    )(q_l, k_l, v_l, r_l, sm_l, k_c, v_c, sm_c, gw_hi, gw_lo, gb, norm_g.reshape(1, dv),
      _block_tri(blk), jnp.asarray(tril), jnp.asarray(tril.T))


def _merge_up_kernel(y_ref, z_ref, o_ref, gt_ref, x_ref, g1_ref, sh2_ref, sc2_ref,
                     sg_ref, n2_ref, wbs_ref, wbg_ref, wo_ref, wup_ref,
                     x1_ref, u_ref):
    y = y_ref[0] * _silu(z_ref[0].astype(F32))
    y = y * lax.rsqrt(jnp.mean(y * y, axis=-1, keepdims=True) + EPS) * sg_ref[...]
    ys = _dot(y.astype(BF16), wbs_ref[...])
    os_ = _dot(o_ref[0], wbg_ref[...])
    gt = gt_ref[0].astype(F32)
    m = gt[:, :D_MODEL] * ys + gt[:, D_MODEL:] * os_
    out = _dot(m.astype(BF16), wo_ref[...])
    x1 = x_ref[0] + g1_ref[0] * out
    x1_ref[0] = x1
    h2 = x1 * lax.rsqrt(jnp.mean(x1 * x1, axis=-1, keepdims=True) + EPS) * n2_ref[...]
    h2 = h2 * (1.0 + sc2_ref[0]) + sh2_ref[0]
    u_ref[0] = _dot(h2.astype(BF16), wup_ref[...]).astype(u_ref.dtype)


def _merge_up(y_raw, z, o, gates, x, gate1, shift2, scale2, ssd_g, norm2_g, w_bs, w_bg, w_o, w_up, tm):
    bsz, seq, _ = x.shape
    n_up = w_up.shape[1]
    tok = lambda w: pl.BlockSpec((1, tm, w), lambda b, i: (b, i, 0))
    mod = pl.BlockSpec((1, 1, D_MODEL), lambda b, i: (b, 0, 0))
    return pl.pallas_call(
        _merge_up_kernel,
        out_shape=[jax.ShapeDtypeStruct((bsz, seq, D_MODEL), F32),
                   jax.ShapeDtypeStruct((bsz, seq, n_up), BF16)],
        grid=(bsz, seq // tm),
        in_specs=[tok(D_MODEL), tok(D_MODEL), tok(D_MODEL), tok(2 * D_MODEL), tok(D_MODEL),
                  mod, mod, mod, _const_spec((1, D_MODEL)), _const_spec((1, D_MODEL)),
                  _const_spec(w_bs.shape), _const_spec(w_bg.shape), _const_spec(w_o.shape),
                  _const_spec(w_up.shape)],
        out_specs=[tok(D_MODEL), tok(n_up)],
        compiler_params=_cparams(2),
        name="merge_up",
    )(y_raw, z, o, gates, x, gate1, shift2, scale2, ssd_g, norm2_g, w_bs, w_bg, w_o, w_up)


FFN_TILE_ROWS = 8
FFN_CBLK = 256


def _ffn_down_kernel(n_row_tiles,
                     um_ref, ut_ref, ub_ref, x1_ref, g2_ref, cw_ref, cb_ref, wd_ref, fg_ref,
                     o_ref,
                     sh_g, sh_v, act_s):
    i = pl.program_id(1)
    tr, gw, cb = FFN_TILE_ROWS, GRID_W, FFN_CBLK
    top_on = (i > 0).astype(F32)
    bot_on = (i < n_row_tiles - 1).astype(F32)
    rid = lax.broadcasted_iota(jnp.int32, (gw, cb), 0)

    def put_shifted(dst, slab_idx, s):
        r0 = pl.multiple_of(slab_idx * gw, gw)
        dst[0, pl.ds(r0, gw), :] = jnp.where(rid == 0, 0.0, pltpu.roll(s, 1, 0))
        dst[1, pl.ds(r0, gw), :] = s
        dst[2, pl.ds(r0, gw), :] = jnp.where(rid == gw - 1, 0.0, pltpu.roll(s, gw - 1, 0))

    for jb in range(D_FF // cb):
        for dst, c0 in ((sh_g, jb * cb), (sh_v, D_FF + jb * cb)):
            put_shifted(dst, 0, ut_ref[0, :, c0:c0 + cb].astype(F32) * top_on)
            put_shifted(dst, tr + 1, ub_ref[0, :, c0:c0 + cb].astype(F32) * bot_on)

            def fill(s, carry, dst=dst, c0=c0):
                r0 = pl.multiple_of(s * gw, gw)
                put_shifted(dst, s + 1, um_ref[0, pl.ds(r0, gw), c0:c0 + cb].astype(F32))
                return carry

            lax.fori_loop(0, tr, fill, 0)

        def conv_row(rr, carry, jb=jb):
            def conv(dst, c0):
                acc = jnp.broadcast_to(cb_ref[0:1, c0:c0 + cb], (gw, cb))
                for dr in range(3):
                    r0 = pl.multiple_of((rr + dr) * gw, gw)
                    for dc in range(3):
                        t = dr * 3 + dc
                        acc = acc + cw_ref[t:t + 1, c0:c0 + cb] * dst[dc, pl.ds(r0, gw), :]
                return acc

            gate = conv(sh_g, jb * cb)
            val = conv(sh_v, D_FF + jb * cb)
            o0 = pl.multiple_of(rr * gw, gw)
            act_s[pl.ds(o0, gw), jb * cb:(jb + 1) * cb] = (_silu(gate) * val).astype(BF16)
            return carry

        lax.fori_loop(0, tr, conv_row, 0)

    ffn = _dot(act_s[...], wd_ref[...])
    x2 = x1_ref[0] + g2_ref[0] * ffn
    o_ref[0] = x2 * lax.rsqrt(jnp.mean(x2 * x2, axis=-1, keepdims=True) + EPS) * fg_ref[...]


def _ffn_down(u, x1, gate2, conv_w, conv_b, w_down, final_g):
    bsz, seq, n_up = u.shape
    n_rows = seq // GRID_W
    tr = FFN_TILE_ROWS
    n_tiles = n_rows // tr
    tm = tr * GRID_W
    return pl.pallas_call(
        functools.partial(_ffn_down_kernel, n_tiles),
        out_shape=jax.ShapeDtypeStruct((bsz, seq, D_MODEL), F32),
        grid=(bsz, n_tiles),
        in_specs=[
            pl.BlockSpec((1, tm, n_up), lambda b, i: (b, i, 0)),
            pl.BlockSpec((1, GRID_W, n_up), lambda b, i: (b, jnp.maximum(i * tr - 1, 0), 0)),
            pl.BlockSpec((1, GRID_W, n_up), lambda b, i: (b, jnp.minimum((i + 1) * tr, n_rows - 1), 0)),
            pl.BlockSpec((1, tm, D_MODEL), lambda b, i: (b, i, 0)),
            pl.BlockSpec((1, 1, D_MODEL), lambda b, i: (b, 0, 0)),
            _const_spec(conv_w.shape), _const_spec(conv_b.shape),
            _const_spec(w_down.shape), _const_spec((1, D_MODEL)),
        ],
        out_specs=pl.BlockSpec((1, tm, D_MODEL), lambda b, i: (b, i, 0)),
        scratch_shapes=[
            pltpu.VMEM((3, (tr + 2) * GRID_W, FFN_CBLK), F32),
            pltpu.VMEM((3, (tr + 2) * GRID_W, FFN_CBLK), F32),
            pltpu.VMEM((tm, D_FF), BF16),
        ],
        compiler_params=_cparams(2),
        name="ffn_down",
    )(u, u, u, x1, gate2, conv_w, conv_b, w_down, final_g)


def kernel(x, c, ctx, c_ctx, w_ada, b_ada, norm1_g, w_in, ssd_conv_w, ssd_conv_b, ssd_dt_bias,
           ssd_a_log, ssd_d, ssd_norm_g, gla_gate_w, gla_gate_b, gla_norm_g, w_br_ssd, w_br_gla,
           w_merge, b_merge, w_o, norm2_g, w_up, ffn_conv_w, ffn_conv_b, w_down, final_norm_g):
    bsz, seq, _ = x.shape
    assert w_ada.shape[0] == 1, "single-layer trunk"
    row = lambda v: v.reshape(1, -1)

    n_cond = bsz + 1
    pad = (-n_cond) % 8
    cond = jnp.concatenate([c, c_ctx[None], jnp.zeros((pad, D_MODEL), F32)], axis=0)
    mod = _adaln(cond, w_ada[0], row(b_ada[0]))
    mx = mod[:bsz].reshape(bsz, N_MOD, 1, D_MODEL)
    mc = mod[bsz:bsz + 1].reshape(1, N_MOD, 1, D_MODEL)

    wi = w_in[0]
    o_z, o_xbc = 0, SSD_D_INNER
    o_dt = o_xbc + SSD_D_INNER + 2 * SSD_GROUPS * SSD_STATE
    o_q = o_dt + DT_COLS
    o_k = o_q + GLA_KEY_DIM
    o_v = o_k + GLA_KEY_DIM
    o_g = o_v + D_MODEL
    o_r = o_g + GLA_GATE_RANK
    w_small = jnp.zeros((D_MODEL, SMALL_W), F32)
    w_small = w_small.at[:, :DT_COLS].set(wi[:, o_dt:o_q])
    w_small = w_small.at[:, DT_COLS:DT_COLS + GLA_GATE_RANK].set(wi[:, o_g:o_r])
    bf = lambda w: w.astype(BF16)
    w_z, w_xbc = bf(wi[:, o_z:o_xbc]), bf(wi[:, o_xbc:o_dt])
    w_q, w_k, w_v, w_r = bf(wi[:, o_q:o_k]), bf(wi[:, o_k:o_v]), bf(wi[:, o_v:o_g]), bf(wi[:, o_r:])
    w_small = bf(w_small)
    bm = row(b_merge[0])
    g1 = row(norm1_g[0])

    xbc_c, k_c, v_c, sm_c = _inproj(
        ctx, mc[:, 0], mc[:, 1], g1, bm, [w_xbc, w_k, w_v, w_small],
        ["none"] * 4, [BF16, BF16, BF16, F32], tm=ctx.shape[1])
    z, xbc, q, k, v, r, gates, sm = _inproj(
        x, mx[:, 0], mx[:, 1], g1, bm,
        [w_z, w_xbc, w_q, w_k, w_v, w_r, bf(w_merge[0]), w_small],
        ["none"] * 6 + ["sigmoid_bias", "none"],
        [BF16] * 7 + [F32], tm=512)

    y_raw = _ssd(xbc, sm, xbc_c, sm_c, ssd_conv_w[0], row(ssd_conv_b[0]),
                 ssd_dt_bias[0], ssd_a_log[0], ssd_d[0])
    o = _gla(q, k, v, r, sm, k_c, v_c, sm_c, gla_gate_w[0], gla_gate_b[0], gla_norm_g[0])

    x1, u = _merge_up(y_raw, z, o, gates, x, mx[:, 2], mx[:, 3], mx[:, 4],
                      row(ssd_norm_g[0]), row(norm2_g[0]),
                      bf(w_br_ssd[0]), bf(w_br_gla[0]), bf(w_o[0]), bf(w_up[0]), tm=512)
    return _ffn_down(u, x1, mx[:, 5], ffn_conv_w[0].reshape(9, -1), row(ffn_conv_b[0]),
                     bf(w_down[0]), row(final_norm_g))
```

```python
import functools
import math

import numpy as np
import jax
import jax.numpy as jnp
from jax import lax
from jax.experimental import pallas as pl
from jax.experimental.pallas import tpu as pltpu

F32 = jnp.float32
BF16 = jnp.bfloat16

D_MODEL = 1024
GRID_W = 64
CHUNK = 64
EPS = 1e-6
N_MOD = 6

SSD_HEAD_DIM = 64
SSD_HEADS = 16
SSD_GROUPS = 4
SSD_STATE = 128
SSD_GROUP_W = (SSD_HEADS // SSD_GROUPS) * SSD_HEAD_DIM
SSD_D_INNER = D_MODEL

GLA_HEADS = 4
GLA_DK = 128
GLA_DV = 256
GLA_GATE_RANK = 16
GLA_GATE_NORM = 16.0
GLA_KEY_DIM = GLA_HEADS * GLA_DK

D_FF = 2816
SMALL_W = 128
DT_COLS = 2 * SSD_HEADS
NEG_BIG = -1e30
MXU_DIM = 256
SUB_ROWS = 256
SCAN_UNROLL = 8
BLOCK_UNROLL = 4

VMEM_LIMIT_BYTES = 56 * 1024 * 1024


def _cparams(n_grid):
    return pltpu.CompilerParams(
        dimension_semantics=("arbitrary",) * n_grid,
        vmem_limit_bytes=VMEM_LIMIT_BYTES,
    )


def _const_spec(shape):
    nd = len(shape)
    return pl.BlockSpec(tuple(shape), lambda *_: (0,) * nd, pipeline_mode=pl.Buffered(1))


def _silu(v):
    return v * jax.nn.sigmoid(v)


def _softplus(v):
    return jnp.maximum(v, 0.0) + jnp.log1p(jnp.exp(-jnp.abs(v)))


def _split2(v):
    hi = v.astype(BF16)
    lo = (v - hi.astype(F32)).astype(BF16)
    return hi, lo


def _dot(a, b):
    return jnp.dot(a, b, preferred_element_type=F32)


def _dot_tn(a, b):
    return lax.dot_general(a, b, (((0,), (0,)), ((), ())), preferred_element_type=F32)


def _dot_nt(a, b):
    return lax.dot_general(a, b, (((1,), (1,)), ((), ())), preferred_element_type=F32)


def _scan_block_rows(seq, seq_c):
    return math.gcd(math.gcd(seq, seq_c), MXU_DIM)


def _block_tri(rows):
    idx = np.arange(rows)
    same = (idx[:, None] // CHUNK) == (idx[None, :] // CHUNK)
    return jnp.asarray((same & (idx[:, None] >= idx[None, :])).astype(np.float32), BF16)


def _adaln_kernel(c_ref, w_ref, b_ref, o_ref):
    s_hi, s_lo = _split2(_silu(c_ref[...]))
    w_hi, w_lo = _split2(w_ref[...])
    o_ref[...] = _dot(s_hi, w_hi) + _dot(s_lo, w_hi) + _dot(s_hi, w_lo) + b_ref[...]


def _adaln(cond, w, b):
    rows = cond.shape[0]
    n_out = w.shape[1]
    tn = D_MODEL
    return pl.pallas_call(
        _adaln_kernel,
        out_shape=jax.ShapeDtypeStruct((rows, n_out), F32),
        grid=(n_out // tn,),
        in_specs=[
            pl.BlockSpec((rows, D_MODEL), lambda j: (0, 0)),
            pl.BlockSpec((D_MODEL, tn), lambda j: (0, j)),
            pl.BlockSpec((1, tn), lambda j: (0, j)),
        ],
        out_specs=pl.BlockSpec((rows, tn), lambda j: (0, j)),
        compiler_params=_cparams(1),
        name="adaln",
    )(cond, w, b)


def _inproj_kernel(posts, x_ref, shift_ref, scale_ref, g_ref, bias_ref, *refs):
    n = len(posts)
    w_refs, o_refs = refs[:n], refs[n:]
    sub = math.gcd(x_ref.shape[1], SUB_ROWS)
    for s in range(x_ref.shape[1] // sub):
        rows = slice(s * sub, (s + 1) * sub)
        x = x_ref[0, rows, :]
        ms = jnp.mean(x * x, axis=-1, keepdims=True)
        xn = (x * lax.rsqrt(ms + EPS)) * g_ref[...]
        h = xn * (1.0 + scale_ref[0]) + shift_ref[0]
        hb = h.astype(BF16)
        for w_ref, o_ref, post in zip(w_refs, o_refs, posts):
            acc = _dot(hb, w_ref[...])
            if post == "sigmoid_bias":
                acc = jax.nn.sigmoid(acc + bias_ref[...])
            o_ref[0, rows, :] = acc.astype(o_ref.dtype)


def _inproj(x, shift, scale, g, bias, weights, posts, out_dtypes, tm):
    bsz, seq, _ = x.shape
    mod_map = (lambda b, i: (b, 0, 0)) if shift.shape[0] > 1 else (lambda b, i: (0, 0, 0))
    in_specs = [
        pl.BlockSpec((1, tm, D_MODEL), lambda b, i: (b, i, 0)),
        pl.BlockSpec((1, 1, D_MODEL), mod_map),
        pl.BlockSpec((1, 1, D_MODEL), mod_map),
        _const_spec((1, D_MODEL)),
        _const_spec(bias.shape),
    ] + [_const_spec(w.shape) for w in weights]
    out_shape = [jax.ShapeDtypeStruct((bsz, seq, w.shape[1]), dt) for w, dt in zip(weights, out_dtypes)]
    out_specs = [pl.BlockSpec((1, tm, w.shape[1]), lambda b, i: (b, i, 0)) for w in weights]
    return pl.pallas_call(
        functools.partial(_inproj_kernel, tuple(posts)),
        out_shape=out_shape,
        grid=(bsz, seq // tm),
        in_specs=in_specs,
        out_specs=out_specs,
        compiler_params=_cparams(2),
        name="inproj",
    )(x, shift, scale, g, bias, *weights)


CONV_ROWS = 64
PACK_ROWS = 16


def _conv_silu_into(src_ref, w_ref, b_ref, dst_ref, dst_off, seq):
    width = dst_ref.shape[-1]
    w0, w1, w2, bias = w_ref[0:1, :], w_ref[1:2, :], w_ref[2:3, :], b_ref[...]
    rid = lax.broadcasted_iota(jnp.int32, (CONV_ROWS, width), 0)

    def body(i, carry):
        r0 = pl.multiple_of(i * CONV_ROWS, CONV_ROWS)
        cur = src_ref[0, pl.ds(r0, CONV_ROWS), :].astype(F32)
        p0 = pl.multiple_of(jnp.maximum(r0 - PACK_ROWS, 0), PACK_ROWS)
        n0 = pl.multiple_of(jnp.minimum(r0 + CONV_ROWS, seq - PACK_ROWS), PACK_ROWS)
        prev_row = src_ref[0, pl.ds(p0, PACK_ROWS), :].astype(F32)[PACK_ROWS - 1:PACK_ROWS, :]
        next_row = src_ref[0, pl.ds(n0, PACK_ROWS), :].astype(F32)[0:1, :]
        prev_row = jnp.where(i > 0, prev_row, 0.0)
        next_row = jnp.where(r0 + CONV_ROWS < seq, next_row, 0.0)
        xm1 = jnp.where(rid == 0, prev_row, pltpu.roll(cur, 1, 0))
        xp1 = jnp.where(rid == CONV_ROWS - 1, next_row, pltpu.roll(cur, CONV_ROWS - 1, 0))
        y = w0 * xm1 + w1 * cur + w2 * xp1 + bias
        d0 = pl.multiple_of(dst_off + r0, CONV_ROWS)
        dst_ref[pl.ds(d0, CONV_ROWS), :] = _silu(y).astype(dst_ref.dtype)
        return carry

    lax.fori_loop(0, seq // CONV_ROWS, body, 0, unroll=BLOCK_UNROLL)


def _bwd_chunk_order(i, nc_c, nct):
    return jnp.where(i < nc_c, nc_c - 1 - i, nct - 1 - (i - nc_c))


def _ssd_kernel(seq, seq_c, blk,
                xs_l, bm_l, cm_l, sm_l, xs_c, bm_c, sm_c,
                cwx, cwb, cwc, cbx, cbb, cbc,
                dtb_ref, e_ref, alog_ref, dskip_ref,
                tri_ref, itile_ref, maskf_ref, maskb_ref, bd_ref,
                y_ref,
                xs_s, bm_s, cm_s, dt_s, dtf_s, dtb_s, cumf_s, cumb_s,
                sf_s, sb_s, ef_s, eb_s, hf_s, hb_s, hrun_f, hrun_b):
    nc, nc_c = seq // CHUNK, seq_c // CHUNK
    nct = nc + nc_c
    seq_t = seq + seq_c
    gw = SSD_GROUP_W
    cpb = blk // CHUNK

    _conv_silu_into(xs_c, cwx, cbx, xs_s, 0, seq_c)
    _conv_silu_into(bm_c, cwb, cbb, bm_s, 0, seq_c)
    _conv_silu_into(xs_l, cwx, cbx, xs_s, seq_c, seq)
    _conv_silu_into(bm_l, cwb, cbb, bm_s, seq_c, seq)
    _conv_silu_into(cm_l, cwc, cbc, cm_s, 0, seq)

    def dt_rows(src_ref, dst_off, n_rows):
        def body(i, carry):
            r0 = pl.multiple_of(i * blk, blk)
            d0 = pl.multiple_of(dst_off + r0, CHUNK)
            dt_s[pl.ds(d0, blk), :] = _softplus(src_ref[0, pl.ds(r0, blk), :] + dtb_ref[...])
            return carry
        lax.fori_loop(0, n_rows // blk, body, 0, unroll=BLOCK_UNROLL)

    dt_rows(sm_c, 0, seq_c)
    dt_rows(sm_l, seq_c, seq)

    def expand(i, carry):
        rows = pl.ds(pl.multiple_of(i * blk, blk), blk)
        hi, lo = _split2(dt_s[rows, :])
        dtf_s[rows, :] = _dot(hi, e_ref[0, 0]) + _dot(lo, e_ref[0, 0])
        dtb_s[rows, :] = _dot(hi, e_ref[0, 1]) + _dot(lo, e_ref[0, 1])
        return carry

    lax.fori_loop(0, seq_t // blk, expand, 0, unroll=BLOCK_UNROLL)

    a_f = -jnp.exp(alog_ref[0, 0:1, :])
    a_b = -jnp.exp(alog_ref[0, 1:2, :])
    tri = tri_ref[...]

    def cumsums(i, carry):
        r0 = pl.multiple_of(i * blk, blk)
        rows = pl.ds(r0, blk)
        da_b = dtb_s[rows, :] * a_b
        hi, lo = _split2(jnp.concatenate([dtf_s[rows, :] * a_f, da_b], axis=1))
        pre = _dot(tri, hi) + _dot(tri, lo)
        cumf_s[rows, :] = pre[:, :gw]
        for c in range(cpb):
            lo_r, hi_r = c * CHUNK, (c + 1) * CHUNK
            tot = pre[hi_r - 1:hi_r, gw:]
            cumb_s[pl.ds(r0 + lo_r, CHUNK), :] = tot - pre[lo_r:hi_r, gw:] + da_b[lo_r:hi_r, :]
        return carry

    lax.fori_loop(0, seq_t // blk, cumsums, 0, unroll=BLOCK_UNROLL)

    def states(j, carry):
        rows = pl.ds(pl.multiple_of(j * CHUNK, CHUNK), CHUNK)
        cum_f, cum_b = cumf_s[rows, :], cumb_s[rows, :]
        last_f = cum_f[CHUNK - 1:CHUNK, :]
        tot_b = cum_b[0:1, :]
        xs = xs_s[rows, :]
        w_f = (xs * (dtf_s[rows, :] * jnp.exp(last_f - cum_f))).astype(BF16)
        w_b = (xs * (dtb_s[rows, :] * jnp.exp(tot_b - cum_b))).astype(BF16)
        st = _dot_tn(bm_s[rows, :], jnp.concatenate([w_f, w_b], axis=1))
        sf_s[j] = st[:, :gw].astype(BF16)
        sb_s[j] = st[:, gw:].astype(BF16)
        ef_s[j] = jnp.broadcast_to(jnp.exp(last_f), (8, gw))
        eb_s[j] = jnp.broadcast_to(jnp.exp(tot_b), (8, gw))
        return carry

    lax.fori_loop(0, nct, states, 0, unroll=SCAN_UNROLL)

    hrun_f[...] = jnp.zeros_like(hrun_f)
    hrun_b[...] = jnp.zeros_like(hrun_b)

    def recur(i, carry):
        h = hrun_f[...]
        hf_s[i] = h.astype(BF16)
        hrun_f[...] = ef_s[i][0:1, :] * h + sf_s[i].astype(F32)
        jb = _bwd_chunk_order(i, nc_c, nct)
        g = hrun_b[...]
        hb_s[jb] = g.astype(BF16)
        hrun_b[...] = eb_s[jb][0:1, :] * g + sb_s[jb].astype(F32)
        return carry

    lax.fori_loop(0, nct, recur, 0, unroll=BLOCK_UNROLL)

    itile = itile_ref[...]
    dskip = dskip_ref[0]

    def rowvec(v):
        return jnp.sum(v * itile, axis=0, keepdims=True)

    def lat_out(j, carry):
        r0 = pl.multiple_of(j * CHUNK, CHUNK)
        rows = pl.ds(r0, CHUNK)
        trow = pl.ds(pl.multiple_of(seq_c + r0, CHUNK), CHUNK)
        xs = xs_s[trow, :]
        bc = bm_s[trow, :]
        cc = cm_s[rows, :]
        cb4 = _dot_nt(cc, jnp.concatenate([bc, bc, bc, bc], axis=0))
        cum_f, cum_b = cumf_s[trow, :], cumb_s[trow, :]
        lf = jnp.exp(cum_f - rowvec(cum_f) + maskf_ref[...]) * rowvec(dtf_s[trow, :])
        lb = jnp.exp(cum_b - rowvec(cum_b) + maskb_ref[...]) * rowvec(dtb_s[trow, :])
        m = (cb4 * (lf + lb)).astype(BF16)
        xb = xs.astype(BF16)
        xbd = jnp.concatenate([xb, xb, xb, xb], axis=0) * bd_ref[...]
        y = _dot(m, xbd)
        y = y + jnp.exp(cum_f) * _dot(cc, hf_s[nc_c + j]) + jnp.exp(cum_b) * _dot(cc, hb_s[nc_c + j])
        y_ref[0, rows, :] = y + dskip * xs
        return carry

    lax.fori_loop(0, nc, lat_out, 0, unroll=SCAN_UNROLL)


def _ssd_consts():
    q = CHUNK
    idx = np.arange(q)
    tri = (idx[:, None] >= idx[None, :]).astype(np.float32)
    itile = np.tile(np.eye(q, dtype=np.float32), (1, SSD_GROUP_W // q))
    maskf = np.tile((1.0 - tri) * NEG_BIG, (1, SSD_GROUP_W // q)).astype(np.float32)
    maskb = np.tile((1.0 - tri.T) * NEG_BIG, (1, SSD_GROUP_W // q)).astype(np.float32)
    blk = np.arange(SSD_GROUP_W) // SSD_HEAD_DIM
    bd = (blk[:, None] == blk[None, :]).astype(np.float32)
    e = np.zeros((SSD_GROUPS, 2, SMALL_W, SSD_GROUP_W), np.float32)
    for g in range(SSD_GROUPS):
        for d in range(2):
            for r in range(SSD_GROUPS):
                e[g, d, d * SSD_HEADS + g * 4 + r, r * SSD_HEAD_DIM:(r + 1) * SSD_HEAD_DIM] = 1.0
    return (jnp.asarray(itile), jnp.asarray(maskf), jnp.asarray(maskb),
            jnp.asarray(bd, BF16), jnp.asarray(e, BF16))


def _ssd(xbc_l, sm_l, xbc_c, sm_c, conv_w, conv_b, dt_bias, a_log, d_skip):
    bsz, seq, _ = xbc_l.shape
    seq_c = xbc_c.shape[1]
    seq_t = seq + seq_c
    nct = seq_t // CHUNK
    blk = _scan_block_rows(seq, seq_c)
    gw, ns, ng = SSD_GROUP_W, SSD_STATE, SSD_GROUPS
    itile, maskf, maskb, bd, e = _ssd_consts()
    tri = _block_tri(blk)
    dtb = jnp.zeros((1, SMALL_W), F32).at[0, :DT_COLS].set(dt_bias.reshape(-1))
    alog = jnp.repeat(a_log.reshape(2, ng, 4), SSD_HEAD_DIM, axis=-1).transpose(1, 0, 2)
    dsk = jnp.repeat(d_skip.reshape(ng, 4), SSD_HEAD_DIM, axis=-1).reshape(ng, 1, gw)
    b_off = SSD_D_INNER // ns
    c_off = b_off + ng
    in_specs = [
        pl.BlockSpec((1, seq, gw), lambda b, g: (b, 0, g)),
        pl.BlockSpec((1, seq, ns), lambda b, g: (b, 0, b_off + g)),
        pl.BlockSpec((1, seq, ns), lambda b, g: (b, 0, c_off + g)),
        pl.BlockSpec((1, seq, SMALL_W), lambda b, g: (b, 0, 0)),
        pl.BlockSpec((1, seq_c, gw), lambda b, g: (b, 0, g)),
        pl.BlockSpec((1, seq_c, ns), lambda b, g: (b, 0, b_off + g)),
        pl.BlockSpec((1, seq_c, SMALL_W), lambda b, g: (b, 0, 0)),
        pl.BlockSpec((3, gw), lambda b, g: (0, g)),
        pl.BlockSpec((3, ns), lambda b, g: (0, b_off + g)),
        pl.BlockSpec((3, ns), lambda b, g: (0, c_off + g)),
        pl.BlockSpec((1, gw), lambda b, g: (0, g)),
        pl.BlockSpec((1, ns), lambda b, g: (0, b_off + g)),
        pl.BlockSpec((1, ns), lambda b, g: (0, c_off + g)),
        _const_spec((1, SMALL_W)),
        pl.BlockSpec((1, 2, SMALL_W, gw), lambda b, g: (g, 0, 0, 0)),
        pl.BlockSpec((1, 2, gw), lambda b, g: (g, 0, 0)),
        pl.BlockSpec((1, 1, gw), lambda b, g: (g, 0, 0)),
        _const_spec(tri.shape), _const_spec(itile.shape), _const_spec(maskf.shape),
        _const_spec(maskb.shape), _const_spec(bd.shape),
    ]
    scratch = [
        pltpu.VMEM((seq_t, gw), F32),
        pltpu.VMEM((seq_t, ns), BF16),
        pltpu.VMEM((seq, ns), BF16),
        pltpu.VMEM((seq_t, SMALL_W), F32),
        pltpu.VMEM((seq_t, gw), F32),
        pltpu.VMEM((seq_t, gw), F32),
        pltpu.VMEM((seq_t, gw), F32),
        pltpu.VMEM((seq_t, gw), F32),
        pltpu.VMEM((nct, ns, gw), BF16),
        pltpu.VMEM((nct, ns, gw), BF16),
        pltpu.VMEM((nct, 8, gw), F32),
        pltpu.VMEM((nct, 8, gw), F32),
        pltpu.VMEM((nct, ns, gw), BF16),
        pltpu.VMEM((nct, ns, gw), BF16),
        pltpu.VMEM((ns, gw), F32),
        pltpu.VMEM((ns, gw), F32),
    ]
    return pl.pallas_call(
        functools.partial(_ssd_kernel, seq, seq_c, blk),
        out_shape=jax.ShapeDtypeStruct((bsz, seq, SSD_D_INNER), F32),
        grid=(bsz, ng),
        in_specs=in_specs,
        out_specs=pl.BlockSpec((1, seq, gw), lambda b, g: (b, 0, g)),
        scratch_shapes=scratch,
        compiler_params=_cparams(2),
        name="ssd",
    )(xbc_l, xbc_l, xbc_l, sm_l, xbc_c, xbc_c, sm_c,
      conv_w, conv_w, conv_w, conv_b, conv_b, conv_b,
      dtb, e, alog, dsk, tri, itile, maskf, maskb, bd)


def _gla_kernel(seq, seq_c, blk,
                q_l, k_l, v_l, sm_l, k_c, v_c, sm_c,
                gwh_ref, gwl_ref, gb_ref, tri_ref, tril_ref, triu_ref,
                o_ref,
                cum_s, la_s, st_s, e_s, hs_s, hrun):
    nc, nc_c = seq // CHUNK, seq_c // CHUNK
    nct = nc + nc_c
    dk = GLA_DK
    cpb = blk // CHUNK
    tri = tri_ref[...]
    scale = GLA_DK ** -0.5
    mid = CHUNK // 2

    def logit_rows(sm_ref, dst_off, n_rows):
        def body(i, carry):
            r0 = pl.multiple_of(i * blk, blk)
            hi, lo = _split2(sm_ref[0, pl.ds(r0, blk), :])
            d0 = pl.multiple_of(dst_off + r0, CHUNK)
            cum_s[pl.ds(d0, blk), :] = _dot(hi, gwh_ref[0]) + _dot(lo, gwh_ref[0]) + _dot(hi, gwl_ref[0])
            return carry
        lax.fori_loop(0, n_rows // blk, body, 0, unroll=BLOCK_UNROLL)

    logit_rows(sm_c, 0, seq_c)
    logit_rows(sm_l, seq_c, seq)

    def log_gate(j, carry):
        rows = pl.ds(pl.multiple_of(j * CHUNK, CHUNK), CHUNK)
        x = cum_s[rows, :] + gb_ref[0]
        la_s[rows, :] = (jnp.minimum(x, 0.0) - jnp.log1p(jnp.exp(-jnp.abs(x)))) * (1.0 / GLA_GATE_NORM)
        return carry

    lax.fori_loop(0, nct, log_gate, 0, unroll=SCAN_UNROLL)

    def cumsums(i, carry):
        r0 = pl.multiple_of(i * blk, blk)
        la = la_s[pl.ds(r0, blk), :]
        lhi, llo = _split2(la)
        pre = _dot(tri, lhi) + _dot(tri, llo)
        cum_s[pl.ds(r0, blk), 0:dk] = pre[:, :dk]
        for c in range(cpb):
            lo_r, hi_r = c * CHUNK, (c + 1) * CHUNK
            tot = pre[hi_r - 1:hi_r, dk:]
            cum_s[pl.ds(r0 + lo_r, CHUNK), dk:2 * dk] = tot - pre[lo_r:hi_r, dk:] + la[lo_r:hi_r, dk:]
        return carry

    lax.fori_loop(0, (seq + seq_c) // blk, cumsums, 0, unroll=BLOCK_UNROLL)

    def state_rows(k_ref, v_ref, chunk_off, n_chunks):
        def body(j, carry):
            rows = pl.ds(pl.multiple_of(j * CHUNK, CHUNK), CHUNK)
            trow = pl.ds(pl.multiple_of((chunk_off + j) * CHUNK, CHUNK), CHUNK)
            cum = cum_s[trow, :]
            cum_f, cum_b = cum[:, :dk], cum[:, dk:]
            last_f = cum_f[CHUNK - 1:CHUNK, :]
            tot_b = cum_b[0:1, :]
            k = k_ref[0, rows, :].astype(F32)
            kd = jnp.concatenate([k * jnp.exp(last_f - cum_f), k * jnp.exp(tot_b - cum_b)], axis=1)
            st_s[chunk_off + j] = _dot_tn(v_ref[0, rows, :], kd.astype(BF16)).astype(BF16)
            e_s[chunk_off + j] = jnp.broadcast_to(
                jnp.concatenate([jnp.exp(last_f), jnp.exp(tot_b)], axis=1), (8, 2 * dk))
            return carry
        lax.fori_loop(0, n_chunks, body, 0, unroll=SCAN_UNROLL)

    state_rows(k_c, v_c, 0, nc_c)
    state_rows(k_l, v_l, nc_c, nc)

    hrun[...] = jnp.zeros_like(hrun)

    def recur(i, carry):
        jb = _bwd_chunk_order(i, nc_c, nct)
        h = hrun[...]
        hs_s[i, :, 0:dk] = h[:, :dk].astype(BF16)
        hs_s[jb, :, dk:2 * dk] = h[:, dk:].astype(BF16)
        dec = jnp.concatenate([e_s[i][0:1, :dk], e_s[jb][0:1, dk:]], axis=1)
        inc = jnp.concatenate([st_s[i, :, 0:dk], st_s[jb, :, dk:2 * dk]], axis=1).astype(F32)
        hrun[...] = dec * h + inc
        return carry

    lax.fori_loop(0, nct, recur, 0, unroll=BLOCK_UNROLL)

    def lat_out(j, carry):
        r0 = pl.multiple_of(j * CHUNK, CHUNK)
        rows = pl.ds(r0, CHUNK)
        trow = pl.ds(pl.multiple_of(seq_c + r0, CHUNK), CHUNK)
        cum = cum_s[trow, :]
        cum_f, cum_b = cum[:, :dk], cum[:, dk:]
        ref_f = cum_f[mid - 1:mid, :]
        ref_b = cum_b[mid:mid + 1, :]
        q = q_l[0, rows, :].astype(F32) * scale
        k = k_l[0, rows, :].astype(F32)
        v = v_l[0, rows, :]
        sc_f = _dot_nt((q * jnp.exp(cum_f - ref_f)).astype(BF16), (k * jnp.exp(ref_f - cum_f)).astype(BF16))
        sc_b = _dot_nt((q * jnp.exp(cum_b - ref_b)).astype(BF16), (k * jnp.exp(ref_b - cum_b)).astype(BF16))
        p = (sc_f * tril_ref[...] + sc_b * triu_ref[...]).astype(BF16)
        qe = jnp.concatenate([q * jnp.exp(cum_f), q * jnp.exp(cum_b)], axis=1).astype(BF16)
        o_ref[0, rows, :] = (_dot(p, v) + _dot_nt(qe, hs_s[nc_c + j])).astype(o_ref.dtype)
        return carry

    lax.fori_loop(0, nc, lat_out, 0, unroll=SCAN_UNROLL)


def _gla(q_l, k_l, v_l, sm_l, k_c, v_c, sm_c, gate_w, gate_b):
    bsz, seq, _ = q_l.shape
    seq_c = k_c.shape[1]
    seq_t = seq + seq_c
    nct = seq_t // CHUNK
    blk = _scan_block_rows(seq, seq_c)
    dk, dv, nh = GLA_DK, GLA_DV, GLA_HEADS
    idx = np.arange(CHUNK)
    tril = (idx[:, None] >= idx[None, :]).astype(np.float32)
    gw = jnp.zeros((nh, SMALL_W, 2 * dk), F32)
    gwh = gate_w.reshape(2, GLA_GATE_RANK, nh, dk).transpose(2, 1, 0, 3).reshape(nh, GLA_GATE_RANK, 2 * dk)
    gw = gw.at[:, DT_COLS:DT_COLS + GLA_GATE_RANK, :].set(gwh)
    gw_hi = gw.astype(BF16)
    gw_lo = (gw - gw_hi.astype(F32)).astype(BF16)
    gb = gate_b.reshape(2, nh, dk).transpose(1, 0, 2).reshape(nh, 1, 2 * dk)
    in_specs = [
        pl.BlockSpec((1, seq, dk), lambda b, h: (b, 0, h)),
        pl.BlockSpec((1, seq, dk), lambda b, h: (b, 0, h)),
        pl.BlockSpec((1, seq, dv), lambda b, h: (b, 0, h)),
        pl.BlockSpec((1, seq, SMALL_W), lambda b, h: (b, 0, 0)),
        pl.BlockSpec((1, seq_c, dk), lambda b, h: (b, 0, h)),
        pl.BlockSpec((1, seq_c, dv), lambda b, h: (b, 0, h)),
        pl.BlockSpec((1, seq_c, SMALL_W), lambda b, h: (b, 0, 0)),
        pl.BlockSpec((1, SMALL_W, 2 * dk), lambda b, h: (h, 0, 0)),
        pl.BlockSpec((1, SMALL_W, 2 * dk), lambda b, h: (h, 0, 0)),
        pl.BlockSpec((1, 1, 2 * dk), lambda b, h: (h, 0, 0)),
        _const_spec((blk, blk)), _const_spec((CHUNK, CHUNK)), _const_spec((CHUNK, CHUNK)),
    ]
    scratch = [
        pltpu.VMEM((seq_t, 2 * dk), F32),
        pltpu.VMEM((seq_t, 2 * dk), F32),
        pltpu.VMEM((nct, dv, 2 * dk), BF16),
        pltpu.VMEM((nct, 8, 2 * dk), F32),
        pltpu.VMEM((nct, dv, 2 * dk), BF16),
        pltpu.VMEM((dv, 2 * dk), F32),
    ]
    return pl.pallas_call(
        functools.partial(_gla_kernel, seq, seq_c, blk),
        out_shape=jax.ShapeDtypeStruct((bsz, seq, nh * dv), BF16),
        grid=(bsz, nh),
        in_specs=in_specs,
        out_specs=pl.BlockSpec((1, seq, dv), lambda b, h: (b, 0, h)),
        scratch_shapes=scratch,
        compiler_params=_cparams(2),
        name="gla",
    )(q_l, k_l, v_l, sm_l, k_c, v_c, sm_c, gw_hi, gw_lo, gb,
      _block_tri(blk), jnp.asarray(tril), jnp.asarray(tril.T))


def _merge_up_kernel(y_ref, z_ref, o_ref, r_ref, gt_ref, x_ref, g1_ref, sh2_ref, sc2_ref,
                     sg_ref, gg_ref, n2_ref, wbs_ref, wbg_ref, wo_ref, wup_ref,
                     x1_ref, u_ref):
    sub = math.gcd(y_ref.shape[1], SUB_ROWS)
    for s in range(y_ref.shape[1] // sub):
        rows = slice(s * sub, (s + 1) * sub)
        y = y_ref[0, rows, :] * _silu(z_ref[0, rows, :].astype(F32))
        y = y * lax.rsqrt(jnp.mean(y * y, axis=-1, keepdims=True) + EPS) * sg_ref[...]
        ys = _dot(y.astype(BF16), wbs_ref[...])
        heads = []
        for h in range(GLA_HEADS):
            oh = o_ref[0, rows, h * GLA_DV:(h + 1) * GLA_DV].astype(F32)
            heads.append(oh * lax.rsqrt(jnp.mean(oh * oh, axis=-1, keepdims=True) + EPS) * gg_ref[...])
        o = jnp.concatenate(heads, axis=1) * _silu(r_ref[0, rows, :].astype(F32))
        os_ = _dot(o.astype(BF16), wbg_ref[...])
        gt = gt_ref[0, rows, :].astype(F32)
        m = gt[:, :D_MODEL] * ys + gt[:, D_MODEL:] * os_
        out = _dot(m.astype(BF16), wo_ref[...])
        x1 = x_ref[0, rows, :] + g1_ref[0] * out
        x1_ref[0, rows, :] = x1
        h2 = x1 * lax.rsqrt(jnp.mean(x1 * x1, axis=-1, keepdims=True) + EPS) * n2_ref[...]
        h2 = h2 * (1.0 + sc2_ref[0]) + sh2_ref[0]
        u_ref[0, rows, :] = _dot(h2.astype(BF16), wup_ref[...]).astype(u_ref.dtype)


def _merge_up(y_raw, z, o, r, gates, x, gate1, shift2, scale2, ssd_g, gla_g, norm2_g,
              w_bs, w_bg, w_o, w_up, tm):
    bsz, seq, _ = x.shape
    n_up = w_up.shape[1]
    tok = lambda w: pl.BlockSpec((1, tm, w), lambda b, i: (b, i, 0))
    mod = pl.BlockSpec((1, 1, D_MODEL), lambda b, i: (b, 0, 0))
    return pl.pallas_call(
        _merge_up_kernel,
        out_shape=[jax.ShapeDtypeStruct((bsz, seq, D_MODEL), F32),
                   jax.ShapeDtypeStruct((bsz, seq, n_up), BF16)],
        grid=(bsz, seq // tm),
        in_specs=[tok(D_MODEL), tok(D_MODEL), tok(D_MODEL), tok(D_MODEL), tok(2 * D_MODEL), tok(D_MODEL),
                  mod, mod, mod, _const_spec((1, D_MODEL)), _const_spec((1, GLA_DV)),
                  _const_spec((1, D_MODEL)),
                  _const_spec(w_bs.shape), _const_spec(w_bg.shape), _const_spec(w_o.shape),
                  _const_spec(w_up.shape)],
        out_specs=[tok(D_MODEL), tok(n_up)],
        compiler_params=_cparams(2),
        name="merge_up",
    )(y_raw, z, o, r, gates, x, gate1, shift2, scale2, ssd_g, gla_g, norm2_g, w_bs, w_bg, w_o, w_up)


FFN_TILE_ROWS = 8
FFN_CBLK = 256


def _ffn_down_kernel(n_row_tiles,
                     um_ref, ut_ref, ub_ref, x1_ref, g2_ref, cw_ref, cb_ref, wd_ref, fg_ref,
                     o_ref,
                     sh_g, sh_v, act_s):
    i = pl.program_id(1)
    tr, gw, cb = FFN_TILE_ROWS, GRID_W, FFN_CBLK
    top_on = jnp.where(i > 0, 1.0, 0.0)
    bot_on = jnp.where(i < n_row_tiles - 1, 1.0, 0.0)
    rid = lax.broadcasted_iota(jnp.int32, (gw, cb), 0)

    def put_shifted(dst, slab_idx, s):
        r0 = pl.multiple_of(slab_idx * gw, gw)
        dst[0, pl.ds(r0, gw), :] = jnp.where(rid == 0, 0.0, pltpu.roll(s, 1, 0))
        dst[1, pl.ds(r0, gw), :] = s
        dst[2, pl.ds(r0, gw), :] = jnp.where(rid == gw - 1, 0.0, pltpu.roll(s, gw - 1, 0))

    for jb in range(D_FF // cb):
        for dst, c0 in ((sh_g, jb * cb), (sh_v, D_FF + jb * cb)):
            put_shifted(dst, 0, ut_ref[0, :, c0:c0 + cb].astype(F32) * top_on)
            put_shifted(dst, tr + 1, ub_ref[0, :, c0:c0 + cb].astype(F32) * bot_on)

            def fill(s, carry, dst=dst, c0=c0):
                r0 = pl.multiple_of(s * gw, gw)
                put_shifted(dst, s + 1, um_ref[0, pl.ds(r0, gw), c0:c0 + cb].astype(F32))
                return carry

            lax.fori_loop(0, tr, fill, 0)

        def conv_row(rr, carry, jb=jb):
            def conv(dst, c0):
                acc = jnp.broadcast_to(cb_ref[:, c0:c0 + cb][None], (gw // 8, 8, cb))
                for dr in range(3):
                    r0 = pl.multiple_of((rr + dr) * gw, gw)
                    for dc in range(3):
                        t = dr * 3 + dc
                        slab = dst[dc, pl.ds(r0, gw), :].reshape(gw // 8, 8, cb)
                        acc = acc + cw_ref[t, :, c0:c0 + cb][None] * slab
                return acc.reshape(gw, cb)

            gate = conv(sh_g, jb * cb)
            val = conv(sh_v, D_FF + jb * cb)
            o0 = pl.multiple_of(rr * gw, gw)
            act_s[pl.ds(o0, gw), jb * cb:(jb + 1) * cb] = (_silu(gate) * val).astype(BF16)
            return carry

        lax.fori_loop(0, tr, conv_row, 0)

    ffn = _dot(act_s[...], wd_ref[...])
    x2 = x1_ref[0] + g2_ref[0] * ffn
    o_ref[0] = x2 * lax.rsqrt(jnp.mean(x2 * x2, axis=-1, keepdims=True) + EPS) * fg_ref[...]


def _ffn_down(u, x1, gate2, conv_w, conv_b, w_down, final_g):
    bsz, seq, n_up = u.shape
    n_rows = seq // GRID_W
    tr = FFN_TILE_ROWS
    n_tiles = n_rows // tr
    tm = tr * GRID_W
    conv_w = jnp.broadcast_to(conv_w[:, None, :], (conv_w.shape[0], 8, n_up))
    conv_b = jnp.broadcast_to(conv_b, (8, n_up))
    return pl.pallas_call(
        functools.partial(_ffn_down_kernel, n_tiles),
        out_shape=jax.ShapeDtypeStruct((bsz, seq, D_MODEL), F32),
        grid=(bsz, n_tiles),
        in_specs=[
            pl.BlockSpec((1, tm, n_up), lambda b, i: (b, i, 0)),
            pl.BlockSpec((1, GRID_W, n_up), lambda b, i: (b, jnp.maximum(i * tr - 1, 0), 0)),
            pl.BlockSpec((1, GRID_W, n_up), lambda b, i: (b, jnp.minimum((i + 1) * tr, n_rows - 1), 0)),
            pl.BlockSpec((1, tm, D_MODEL), lambda b, i: (b, i, 0)),
            pl.BlockSpec((1, 1, D_MODEL), lambda b, i: (b, 0, 0)),
            _const_spec(conv_w.shape), _const_spec(conv_b.shape),
            _const_spec(w_down.shape), _const_spec((1, D_MODEL)),
        ],
        out_specs=pl.BlockSpec((1, tm, D_MODEL), lambda b, i: (b, i, 0)),
        scratch_shapes=[
            pltpu.VMEM((3, (tr + 2) * GRID_W, FFN_CBLK), F32),
            pltpu.VMEM((3, (tr + 2) * GRID_W, FFN_CBLK), F32),
            pltpu.VMEM((tm, D_FF), BF16),
        ],
        compiler_params=_cparams(2),
        name="ffn_down",
    )(u, u, u, x1, gate2, conv_w, conv_b, w_down, final_g)


def kernel(x, c, ctx, c_ctx, w_ada, b_ada, norm1_g, w_in, ssd_conv_w, ssd_conv_b, ssd_dt_bias,
           ssd_a_log, ssd_d, ssd_norm_g, gla_gate_w, gla_gate_b, gla_norm_g, w_br_ssd, w_br_gla,
           w_merge, b_merge, w_o, norm2_g, w_up, ffn_conv_w, ffn_conv_b, w_down, final_norm_g):
    bsz, seq, _ = x.shape
    assert w_ada.shape[0] == 1, "single-layer trunk"
    row = lambda v: v.reshape(1, -1)

    n_cond = bsz + 1
    pad = (-n_cond) % 8
    cond = jnp.concatenate([c, c_ctx[None], jnp.zeros((pad, D_MODEL), F32)], axis=0)
    mod = _adaln(cond, w_ada[0], row(b_ada[0]))
    mx = mod[:bsz].reshape(bsz, N_MOD, 1, D_MODEL)
    mc = mod[bsz:bsz + 1].reshape(1, N_MOD, 1, D_MODEL)

    wi = w_in[0]
    o_z, o_xbc = 0, SSD_D_INNER
    o_dt = o_xbc + SSD_D_INNER + 2 * SSD_GROUPS * SSD_STATE
    o_q = o_dt + DT_COLS
    o_k = o_q + GLA_KEY_DIM
    o_v = o_k + GLA_KEY_DIM
    o_g = o_v + D_MODEL
    o_r = o_g + GLA_GATE_RANK
    w_small = jnp.zeros((D_MODEL, SMALL_W), F32)
    w_small = w_small.at[:, :DT_COLS].set(wi[:, o_dt:o_q])
    w_small = w_small.at[:, DT_COLS:DT_COLS + GLA_GATE_RANK].set(wi[:, o_g:o_r])
    bf = lambda w: w.astype(BF16)
    w_z, w_xbc = bf(wi[:, o_z:o_xbc]), bf(wi[:, o_xbc:o_dt])
    w_q, w_k, w_v, w_r = bf(wi[:, o_q:o_k]), bf(wi[:, o_k:o_v]), bf(wi[:, o_v:o_g]), bf(wi[:, o_r:])
    w_small = bf(w_small)
    bm = row(b_merge[0])
    g1 = row(norm1_g[0])

    xbc_c, k_c, v_c, sm_c = _inproj(
        ctx, mc[:, 0], mc[:, 1], g1, bm, [w_xbc, w_k, w_v, w_small],
        ["none"] * 4, [BF16, BF16, BF16, F32], tm=ctx.shape[1])
    z, xbc, q, k, v, r, gates, sm = _inproj(
        x, mx[:, 0], mx[:, 1], g1, bm,
        [w_z, w_xbc, w_q, w_k, w_v, w_r, bf(w_merge[0]), w_small],
        ["none"] * 6 + ["sigmoid_bias", "none"],
        [BF16] * 7 + [F32], tm=512)

    y_raw = _ssd(xbc, sm, xbc_c, sm_c, ssd_conv_w[0], row(ssd_conv_b[0]),
                 ssd_dt_bias[0], ssd_a_log[0], ssd_d[0])
    o = _gla(q, k, v, sm, k_c, v_c, sm_c, gla_gate_w[0], gla_gate_b[0])

    x1, u = _merge_up(y_raw, z, o, r, gates, x, mx[:, 2], mx[:, 3], mx[:, 4],
                      row(ssd_norm_g[0]), row(gla_norm_g[0]), row(norm2_g[0]),
                      bf(w_br_ssd[0]), bf(w_br_gla[0]), bf(w_o[0]), bf(w_up[0]), tm=512)
    return _ffn_down(u, x1, mx[:, 5], ffn_conv_w[0].reshape(9, -1), row(ffn_conv_b[0]),
                     bf(w_down[0]), row(final_norm_g))
```

```python
import functools
import math

import numpy as np
import jax
import jax.numpy as jnp
from jax import lax
from jax.experimental import pallas as pl
from jax.experimental.pallas import tpu as pltpu

F32 = jnp.float32
BF16 = jnp.bfloat16

D_MODEL = 1024
GRID_W = 64
CHUNK = 64
EPS = 1e-6
N_MOD = 6

SSD_HEAD_DIM = 64
SSD_HEADS = 16
SSD_GROUPS = 4
SSD_STATE = 128
SSD_GROUP_W = (SSD_HEADS // SSD_GROUPS) * SSD_HEAD_DIM
SSD_D_INNER = D_MODEL

GLA_HEADS = 4
GLA_DK = 128
GLA_DV = 256
GLA_GATE_RANK = 16
GLA_GATE_NORM = 16.0
GLA_KEY_DIM = GLA_HEADS * GLA_DK

D_FF = 2816
SMALL_W = 128
DT_COLS = 2 * SSD_HEADS
NEG_BIG = -1e30
MXU_DIM = 256
SUB_ROWS = 256
UP_CHUNK = 512
SCAN_UNROLL = 8
BLOCK_UNROLL = 4

VMEM_LIMIT_BYTES = 56 * 1024 * 1024


def _cparams(n_grid):
    return pltpu.CompilerParams(
        dimension_semantics=("arbitrary",) * n_grid,
        vmem_limit_bytes=VMEM_LIMIT_BYTES,
    )


def _const_spec(shape):
    nd = len(shape)
    return pl.BlockSpec(tuple(shape), lambda *_: (0,) * nd, pipeline_mode=pl.Buffered(1))


def _silu(v):
    return v * jax.nn.sigmoid(v)


def _softplus(v):
    return jnp.maximum(v, 0.0) + jnp.log1p(jnp.exp(-jnp.abs(v)))


def _split2(v):
    hi = v.astype(BF16)
    lo = (v - hi.astype(F32)).astype(BF16)
    return hi, lo


def _dot(a, b):
    return jnp.dot(a, b, preferred_element_type=F32)


def _dot_tn(a, b):
    return lax.dot_general(a, b, (((0,), (0,)), ((), ())), preferred_element_type=F32)


def _dot_nt(a, b):
    return lax.dot_general(a, b, (((1,), (1,)), ((), ())), preferred_element_type=F32)


def _scan_block_rows(seq, seq_c):
    return math.gcd(math.gcd(seq, seq_c), MXU_DIM)


def _block_tri(rows):
    idx = np.arange(rows)
    same = (idx[:, None] // CHUNK) == (idx[None, :] // CHUNK)
    return jnp.asarray((same & (idx[:, None] >= idx[None, :])).astype(np.float32), BF16)


def _adaln_kernel(c_ref, w_ref, b_ref, o_ref):
    s_hi, s_lo = _split2(_silu(c_ref[...]))
    w_hi, w_lo = _split2(w_ref[...])
    o_ref[...] = _dot(s_hi, w_hi) + _dot(s_lo, w_hi) + _dot(s_hi, w_lo) + b_ref[...]


def _adaln(cond, w, b):
    rows = cond.shape[0]
    n_out = w.shape[1]
    tn = D_MODEL
    return pl.pallas_call(
        _adaln_kernel,
        out_shape=jax.ShapeDtypeStruct((rows, n_out), F32),
        grid=(n_out // tn,),
        in_specs=[
            pl.BlockSpec((rows, D_MODEL), lambda j: (0, 0)),
            pl.BlockSpec((D_MODEL, tn), lambda j: (0, j)),
            pl.BlockSpec((1, tn), lambda j: (0, j)),
        ],
        out_specs=pl.BlockSpec((rows, tn), lambda j: (0, j)),
        compiler_params=_cparams(1),
        name="adaln",
    )(cond, w, b)


def _inproj_kernel(posts, x_ref, shift_ref, scale_ref, g_ref, bias_ref, *refs):
    n = len(posts)
    w_refs, o_refs = refs[:n], refs[n:]
    sub = math.gcd(x_ref.shape[1], SUB_ROWS)
    for s in range(x_ref.shape[1] // sub):
        rows = slice(s * sub, (s + 1) * sub)
        x = x_ref[0, rows, :]
        ms = jnp.mean(x * x, axis=-1, keepdims=True)
        xn = (x * lax.rsqrt(ms + EPS)) * g_ref[...]
        h = xn * (1.0 + scale_ref[0]) + shift_ref[0]
        hb = h.astype(BF16)
        for w_ref, o_ref, post in zip(w_refs, o_refs, posts):
            acc = _dot(hb, w_ref[...])
            if post == "sigmoid_bias":
                acc = jax.nn.sigmoid(acc + bias_ref[...])
            o_ref[0, rows, :] = acc.astype(o_ref.dtype)


def _inproj(x, shift, scale, g, bias, weights, posts, out_dtypes, tm):
    bsz, seq, _ = x.shape
    mod_map = (lambda b, i: (b, 0, 0)) if shift.shape[0] > 1 else (lambda b, i: (0, 0, 0))
    in_specs = [
        pl.BlockSpec((1, tm, D_MODEL), lambda b, i: (b, i, 0)),
        pl.BlockSpec((1, 1, D_MODEL), mod_map),
        pl.BlockSpec((1, 1, D_MODEL), mod_map),
        _const_spec((1, D_MODEL)),
        _const_spec(bias.shape),
    ] + [_const_spec(w.shape) for w in weights]
    out_shape = [jax.ShapeDtypeStruct((bsz, seq, w.shape[1]), dt) for w, dt in zip(weights, out_dtypes)]
    out_specs = [pl.BlockSpec((1, tm, w.shape[1]), lambda b, i: (b, i, 0)) for w in weights]
    return pl.pallas_call(
        functools.partial(_inproj_kernel, tuple(posts)),
        out_shape=out_shape,
        grid=(bsz, seq // tm),
        in_specs=in_specs,
        out_specs=out_specs,
        compiler_params=_cparams(2),
        name="inproj",
    )(x, shift, scale, g, bias, *weights)


CONV_ROWS = 64
PACK_ROWS = 16


def _conv_silu_into(src_ref, w_ref, b_ref, dst_ref, dst_off, seq):
    width = dst_ref.shape[-1]
    w0, w1, w2, bias = w_ref[0:1, :], w_ref[1:2, :], w_ref[2:3, :], b_ref[...]
    rid = lax.broadcasted_iota(jnp.int32, (CONV_ROWS, width), 0)

    def body(i, carry):
        r0 = pl.multiple_of(i * CONV_ROWS, CONV_ROWS)
        cur = src_ref[0, pl.ds(r0, CONV_ROWS), :].astype(F32)
        p0 = pl.multiple_of(jnp.maximum(r0 - PACK_ROWS, 0), PACK_ROWS)
        n0 = pl.multiple_of(jnp.minimum(r0 + CONV_ROWS, seq - PACK_ROWS), PACK_ROWS)
        prev_row = src_ref[0, pl.ds(p0, PACK_ROWS), :].astype(F32)[PACK_ROWS - 1:PACK_ROWS, :]
        next_row = src_ref[0, pl.ds(n0, PACK_ROWS), :].astype(F32)[0:1, :]
        prev_row = jnp.where(i > 0, prev_row, 0.0)
        next_row = jnp.where(r0 + CONV_ROWS < seq, next_row, 0.0)
        xm1 = jnp.where(rid == 0, prev_row, pltpu.roll(cur, 1, 0))
        xp1 = jnp.where(rid == CONV_ROWS - 1, next_row, pltpu.roll(cur, CONV_ROWS - 1, 0))
        y = w0 * xm1 + w1 * cur + w2 * xp1 + bias
        d0 = pl.multiple_of(dst_off + r0, CONV_ROWS)
        dst_ref[pl.ds(d0, CONV_ROWS), :] = _silu(y).astype(dst_ref.dtype)
        return carry

    lax.fori_loop(0, seq // CONV_ROWS, body, 0, unroll=BLOCK_UNROLL)


def _bwd_chunk_order(i, nc_c, nct):
    return jnp.where(i < nc_c, nc_c - 1 - i, nct - 1 - (i - nc_c))


def _ssd_kernel(seq, seq_c, blk,
                xs_l, bm_l, cm_l, sm_l, xs_c, bm_c, sm_c,
                cwx, cwb, cwc, cbx, cbb, cbc,
                dtb_ref, e_ref, alog_ref, dskip_ref,
                tri_ref, itile_ref, maskf_ref, maskb_ref, bd_ref,
                y_ref,
                xs_s, bm_s, cm_s, dt_s, dtf_s, dtb_s, cumf_s, cumb_s,
                sf_s, sb_s, ef_s, eb_s, hf_s, hb_s, hrun_f, hrun_b):
    nc, nc_c = seq // CHUNK, seq_c // CHUNK
    nct = nc + nc_c
    seq_t = seq + seq_c
    gw = SSD_GROUP_W
    cpb = blk // CHUNK

    _conv_silu_into(xs_c, cwx, cbx, xs_s, 0, seq_c)
    _conv_silu_into(bm_c, cwb, cbb, bm_s, 0, seq_c)
    _conv_silu_into(xs_l, cwx, cbx, xs_s, seq_c, seq)
    _conv_silu_into(bm_l, cwb, cbb, bm_s, seq_c, seq)
    _conv_silu_into(cm_l, cwc, cbc, cm_s, 0, seq)

    def dt_rows(src_ref, dst_off, n_rows):
        def body(i, carry):
            r0 = pl.multiple_of(i * blk, blk)
            d0 = pl.multiple_of(dst_off + r0, CHUNK)
            dt_s[pl.ds(d0, blk), :] = _softplus(src_ref[0, pl.ds(r0, blk), :] + dtb_ref[...])
            return carry
        lax.fori_loop(0, n_rows // blk, body, 0, unroll=BLOCK_UNROLL)

    dt_rows(sm_c, 0, seq_c)
    dt_rows(sm_l, seq_c, seq)

    def expand(i, carry):
        rows = pl.ds(pl.multiple_of(i * blk, blk), blk)
        hi, lo = _split2(dt_s[rows, :])
        dtf_s[rows, :] = _dot(hi, e_ref[0, 0]) + _dot(lo, e_ref[0, 0])
        dtb_s[rows, :] = _dot(hi, e_ref[0, 1]) + _dot(lo, e_ref[0, 1])
        return carry

    lax.fori_loop(0, seq_t // blk, expand, 0, unroll=BLOCK_UNROLL)

    a_f = -jnp.exp(alog_ref[0, 0:1, :])
    a_b = -jnp.exp(alog_ref[0, 1:2, :])
    tri = tri_ref[...]

    def cumsums(i, carry):
        r0 = pl.multiple_of(i * blk, blk)
        rows = pl.ds(r0, blk)
        da_b = dtb_s[rows, :] * a_b
        hi, lo = _split2(jnp.concatenate([dtf_s[rows, :] * a_f, da_b], axis=1))
        pre = _dot(tri, hi) + _dot(tri, lo)
        cumf_s[rows, :] = pre[:, :gw]
        for c in range(cpb):
            lo_r, hi_r = c * CHUNK, (c + 1) * CHUNK
            tot = pre[hi_r - 1:hi_r, gw:]
            cumb_s[pl.ds(r0 + lo_r, CHUNK), :] = tot - pre[lo_r:hi_r, gw:] + da_b[lo_r:hi_r, :]
        return carry

    lax.fori_loop(0, seq_t // blk, cumsums, 0, unroll=BLOCK_UNROLL)

    def states(j, carry):
        rows = pl.ds(pl.multiple_of(j * CHUNK, CHUNK), CHUNK)
        cum_f, cum_b = cumf_s[rows, :], cumb_s[rows, :]
        last_f = cum_f[CHUNK - 1:CHUNK, :]
        tot_b = cum_b[0:1, :]
        xs = xs_s[rows, :]
        w_f = (xs * (dtf_s[rows, :] * jnp.exp(last_f - cum_f))).astype(BF16)
        w_b = (xs * (dtb_s[rows, :] * jnp.exp(tot_b - cum_b))).astype(BF16)
        st = _dot_tn(bm_s[rows, :], jnp.concatenate([w_f, w_b], axis=1))
        sf_s[j] = st[:, :gw].astype(BF16)
        sb_s[j] = st[:, gw:].astype(BF16)
        ef_s[j] = jnp.broadcast_to(jnp.exp(last_f), (8, gw))
        eb_s[j] = jnp.broadcast_to(jnp.exp(tot_b), (8, gw))
        return carry

    lax.fori_loop(0, nct, states, 0, unroll=SCAN_UNROLL)

    hrun_f[...] = jnp.zeros_like(hrun_f)
    hrun_b[...] = jnp.zeros_like(hrun_b)

    def recur(i, carry):
        h = hrun_f[...]
        hf_s[i] = h.astype(BF16)
        hrun_f[...] = ef_s[i][0:1, :] * h + sf_s[i].astype(F32)
        jb = _bwd_chunk_order(i, nc_c, nct)
        g = hrun_b[...]
        hb_s[jb] = g.astype(BF16)
        hrun_b[...] = eb_s[jb][0:1, :] * g + sb_s[jb].astype(F32)
        return carry

    lax.fori_loop(0, nct, recur, 0, unroll=BLOCK_UNROLL)

    itile = itile_ref[...]
    dskip = dskip_ref[0]

    def rowvec(v):
        return jnp.sum(v * itile, axis=0, keepdims=True)

    def lat_out(j, carry):
        r0 = pl.multiple_of(j * CHUNK, CHUNK)
        rows = pl.ds(r0, CHUNK)
        trow = pl.ds(pl.multiple_of(seq_c + r0, CHUNK), CHUNK)
        xs = xs_s[trow, :]
        bc = bm_s[trow, :]
        cc = cm_s[rows, :]
        cb4 = _dot_nt(cc, jnp.concatenate([bc, bc, bc, bc], axis=0))
        cum_f, cum_b = cumf_s[trow, :], cumb_s[trow, :]
        lf = jnp.exp(cum_f - rowvec(cum_f) + maskf_ref[...]) * rowvec(dtf_s[trow, :])
        lb = jnp.exp(cum_b - rowvec(cum_b) + maskb_ref[...]) * rowvec(dtb_s[trow, :])
        m = (cb4 * (lf + lb)).astype(BF16)
        xb = xs.astype(BF16)
        xbd = jnp.concatenate([xb, xb, xb, xb], axis=0) * bd_ref[...]
        y = _dot(m, xbd)
        y = y + jnp.exp(cum_f) * _dot(cc, hf_s[nc_c + j]) + jnp.exp(cum_b) * _dot(cc, hb_s[nc_c + j])
        y_ref[0, rows, :] = y + dskip * xs
        return carry

    lax.fori_loop(0, nc, lat_out, 0, unroll=SCAN_UNROLL)


def _ssd_consts():
    q = CHUNK
    idx = np.arange(q)
    tri = (idx[:, None] >= idx[None, :]).astype(np.float32)
    itile = np.tile(np.eye(q, dtype=np.float32), (1, SSD_GROUP_W // q))
    maskf = np.tile((1.0 - tri) * NEG_BIG, (1, SSD_GROUP_W // q)).astype(np.float32)
    maskb = np.tile((1.0 - tri.T) * NEG_BIG, (1, SSD_GROUP_W // q)).astype(np.float32)
    blk = np.arange(SSD_GROUP_W) // SSD_HEAD_DIM
    bd = (blk[:, None] == blk[None, :]).astype(np.float32)
    e = np.zeros((SSD_GROUPS, 2, SMALL_W, SSD_GROUP_W), np.float32)
    for g in range(SSD_GROUPS):
        for d in range(2):
            for r in range(SSD_GROUPS):
                e[g, d, d * SSD_HEADS + g * 4 + r, r * SSD_HEAD_DIM:(r + 1) * SSD_HEAD_DIM] = 1.0
    return (jnp.asarray(itile), jnp.asarray(maskf), jnp.asarray(maskb),
            jnp.asarray(bd, BF16), jnp.asarray(e, BF16))


def _ssd(xbc_l, sm_l, xbc_c, sm_c, conv_w, conv_b, dt_bias, a_log, d_skip):
    bsz, seq, _ = xbc_l.shape
    seq_c = xbc_c.shape[1]
    seq_t = seq + seq_c
    nct = seq_t // CHUNK
    blk = _scan_block_rows(seq, seq_c)
    gw, ns, ng = SSD_GROUP_W, SSD_STATE, SSD_GROUPS
    itile, maskf, maskb, bd, e = _ssd_consts()
    tri = _block_tri(blk)
    dtb = jnp.zeros((1, SMALL_W), F32).at[0, :DT_COLS].set(dt_bias.reshape(-1))
    alog = jnp.repeat(a_log.reshape(2, ng, 4), SSD_HEAD_DIM, axis=-1).transpose(1, 0, 2)
    dsk = jnp.repeat(d_skip.reshape(ng, 4), SSD_HEAD_DIM, axis=-1).reshape(ng, 1, gw)
    b_off = SSD_D_INNER // ns
    c_off = b_off + ng
    in_specs = [
        pl.BlockSpec((1, seq, gw), lambda b, g: (b, 0, g)),
        pl.BlockSpec((1, seq, ns), lambda b, g: (b, 0, b_off + g)),
        pl.BlockSpec((1, seq, ns), lambda b, g: (b, 0, c_off + g)),
        pl.BlockSpec((1, seq, SMALL_W), lambda b, g: (b, 0, 0)),
        pl.BlockSpec((1, seq_c, gw), lambda b, g: (b, 0, g)),
        pl.BlockSpec((1, seq_c, ns), lambda b, g: (b, 0, b_off + g)),
        pl.BlockSpec((1, seq_c, SMALL_W), lambda b, g: (b, 0, 0)),
        pl.BlockSpec((3, gw), lambda b, g: (0, g)),
        pl.BlockSpec((3, ns), lambda b, g: (0, b_off + g)),
        pl.BlockSpec((3, ns), lambda b, g: (0, c_off + g)),
        pl.BlockSpec((1, gw), lambda b, g: (0, g)),
        pl.BlockSpec((1, ns), lambda b, g: (0, b_off + g)),
        pl.BlockSpec((1, ns), lambda b, g: (0, c_off + g)),
        _const_spec((1, SMALL_W)),
        pl.BlockSpec((1, 2, SMALL_W, gw), lambda b, g: (g, 0, 0, 0)),
        pl.BlockSpec((1, 2, gw), lambda b, g: (g, 0, 0)),
        pl.BlockSpec((1, 1, gw), lambda b, g: (g, 0, 0)),
        _const_spec(tri.shape), _const_spec(itile.shape), _const_spec(maskf.shape),
        _const_spec(maskb.shape), _const_spec(bd.shape),
    ]
    scratch = [
        pltpu.VMEM((seq_t, gw), F32),
        pltpu.VMEM((seq_t, ns), BF16),
        pltpu.VMEM((seq, ns), BF16),
        pltpu.VMEM((seq_t, SMALL_W), F32),
        pltpu.VMEM((seq_t, gw), F32),
        pltpu.VMEM((seq_t, gw), F32),
        pltpu.VMEM((seq_t, gw), F32),
        pltpu.VMEM((seq_t, gw), F32),
        pltpu.VMEM((nct, ns, gw), BF16),
        pltpu.VMEM((nct, ns, gw), BF16),
        pltpu.VMEM((nct, 8, gw), F32),
        pltpu.VMEM((nct, 8, gw), F32),
        pltpu.VMEM((nct, ns, gw), BF16),
        pltpu.VMEM((nct, ns, gw), BF16),
        pltpu.VMEM((ns, gw), F32),
        pltpu.VMEM((ns, gw), F32),
    ]
    return pl.pallas_call(
        functools.partial(_ssd_kernel, seq, seq_c, blk),
        out_shape=jax.ShapeDtypeStruct((bsz, seq, SSD_D_INNER), F32),
        grid=(bsz, ng),
        in_specs=in_specs,
        out_specs=pl.BlockSpec((1, seq, gw), lambda b, g: (b, 0, g)),
        scratch_shapes=scratch,
        compiler_params=_cparams(2),
        name="ssd",
    )(xbc_l, xbc_l, xbc_l, sm_l, xbc_c, xbc_c, sm_c,
      conv_w, conv_w, conv_w, conv_b, conv_b, conv_b,
      dtb, e, alog, dsk, tri, itile, maskf, maskb, bd)


def _gla_kernel(seq, seq_c, blk,
                q_l, k_l, v_l, sm_l, k_c, v_c, sm_c,
                gwh_ref, gwl_ref, gb_ref, tri_ref, tril_ref, triu_ref,
                o_ref,
                cum_s, la_s, st_s, e_s, hs_s, hrun):
    nc, nc_c = seq // CHUNK, seq_c // CHUNK
    nct = nc + nc_c
    dk = GLA_DK
    cpb = blk // CHUNK
    tri = tri_ref[...]
    scale = GLA_DK ** -0.5
    mid = CHUNK // 2

    def logit_rows(sm_ref, dst_off, n_rows):
        def body(i, carry):
            r0 = pl.multiple_of(i * blk, blk)
            hi, lo = _split2(sm_ref[0, pl.ds(r0, blk), :])
            d0 = pl.multiple_of(dst_off + r0, CHUNK)
            cum_s[pl.ds(d0, blk), :] = _dot(hi, gwh_ref[0]) + _dot(lo, gwh_ref[0]) + _dot(hi, gwl_ref[0])
            return carry
        lax.fori_loop(0, n_rows // blk, body, 0, unroll=BLOCK_UNROLL)

    logit_rows(sm_c, 0, seq_c)
    logit_rows(sm_l, seq_c, seq)

    def log_gate(j, carry):
        rows = pl.ds(pl.multiple_of(j * CHUNK, CHUNK), CHUNK)
        x = cum_s[rows, :] + gb_ref[0]
        la_s[rows, :] = (jnp.minimum(x, 0.0) - jnp.log1p(jnp.exp(-jnp.abs(x)))) * (1.0 / GLA_GATE_NORM)
        return carry

    lax.fori_loop(0, nct, log_gate, 0, unroll=SCAN_UNROLL)

    def cumsums(i, carry):
        r0 = pl.multiple_of(i * blk, blk)
        la = la_s[pl.ds(r0, blk), :]
        lhi, llo = _split2(la)
        pre = _dot(tri, lhi) + _dot(tri, llo)
        cum_s[pl.ds(r0, blk), 0:dk] = pre[:, :dk]
        for c in range(cpb):
            lo_r, hi_r = c * CHUNK, (c + 1) * CHUNK
            tot = pre[hi_r - 1:hi_r, dk:]
            cum_s[pl.ds(r0 + lo_r, CHUNK), dk:2 * dk] = tot - pre[lo_r:hi_r, dk:] + la[lo_r:hi_r, dk:]
        return carry

    lax.fori_loop(0, (seq + seq_c) // blk, cumsums, 0, unroll=BLOCK_UNROLL)

    def state_rows(k_ref, v_ref, chunk_off, n_chunks):
        def body(j, carry):
            rows = pl.ds(pl.multiple_of(j * CHUNK, CHUNK), CHUNK)
            trow = pl.ds(pl.multiple_of((chunk_off + j) * CHUNK, CHUNK), CHUNK)
            cum = cum_s[trow, :]
            cum_f, cum_b = cum[:, :dk], cum[:, dk:]
            last_f = cum_f[CHUNK - 1:CHUNK, :]
            tot_b = cum_b[0:1, :]
            k = k_ref[0, rows, :].astype(F32)
            kd = jnp.concatenate([k * jnp.exp(last_f - cum_f), k * jnp.exp(tot_b - cum_b)], axis=1)
            st_s[chunk_off + j] = _dot_tn(v_ref[0, rows, :], kd.astype(BF16)).astype(BF16)
            e_s[chunk_off + j] = jnp.broadcast_to(
                jnp.concatenate([jnp.exp(last_f), jnp.exp(tot_b)], axis=1), (8, 2 * dk))
            return carry
        lax.fori_loop(0, n_chunks, body, 0, unroll=SCAN_UNROLL)

    state_rows(k_c, v_c, 0, nc_c)
    state_rows(k_l, v_l, nc_c, nc)

    hrun[...] = jnp.zeros_like(hrun)

    def recur(i, carry):
        jb = _bwd_chunk_order(i, nc_c, nct)
        h = hrun[...]
        hs_s[i, :, 0:dk] = h[:, :dk].astype(BF16)
        hs_s[jb, :, dk:2 * dk] = h[:, dk:].astype(BF16)
        dec = jnp.concatenate([e_s[i][0:1, :dk], e_s[jb][0:1, dk:]], axis=1)
        inc = jnp.concatenate([st_s[i, :, 0:dk], st_s[jb, :, dk:2 * dk]], axis=1).astype(F32)
        hrun[...] = dec * h + inc
        return carry

    lax.fori_loop(0, nct, recur, 0, unroll=BLOCK_UNROLL)

    def lat_out(j, carry):
        r0 = pl.multiple_of(j * CHUNK, CHUNK)
        rows = pl.ds(r0, CHUNK)
        trow = pl.ds(pl.multiple_of(seq_c + r0, CHUNK), CHUNK)
        cum = cum_s[trow, :]
        cum_f, cum_b = cum[:, :dk], cum[:, dk:]
        ref_f = cum_f[mid - 1:mid, :]
        ref_b = cum_b[mid:mid + 1, :]
        q = q_l[0, rows, :].astype(F32) * scale
        k = k_l[0, rows, :].astype(F32)
        v = v_l[0, rows, :]
        sc_f = _dot_nt((q * jnp.exp(cum_f - ref_f)).astype(BF16), (k * jnp.exp(ref_f - cum_f)).astype(BF16))
        sc_b = _dot_nt((q * jnp.exp(cum_b - ref_b)).astype(BF16), (k * jnp.exp(ref_b - cum_b)).astype(BF16))
        p = (sc_f * tril_ref[...] + sc_b * triu_ref[...]).astype(BF16)
        qe = jnp.concatenate([q * jnp.exp(cum_f), q * jnp.exp(cum_b)], axis=1).astype(BF16)
        o_ref[0, rows, :] = (_dot(p, v) + _dot_nt(qe, hs_s[nc_c + j])).astype(o_ref.dtype)
        return carry

    lax.fori_loop(0, nc, lat_out, 0, unroll=SCAN_UNROLL)


def _gla(q_l, k_l, v_l, sm_l, k_c, v_c, sm_c, gate_w, gate_b):
    bsz, seq, _ = q_l.shape
    seq_c = k_c.shape[1]
    seq_t = seq + seq_c
    nct = seq_t // CHUNK
    blk = _scan_block_rows(seq, seq_c)
    dk, dv, nh = GLA_DK, GLA_DV, GLA_HEADS
    idx = np.arange(CHUNK)
    tril = (idx[:, None] >= idx[None, :]).astype(np.float32)
    gw = jnp.zeros((nh, SMALL_W, 2 * dk), F32)
    gwh = gate_w.reshape(2, GLA_GATE_RANK, nh, dk).transpose(2, 1, 0, 3).reshape(nh, GLA_GATE_RANK, 2 * dk)
    gw = gw.at[:, DT_COLS:DT_COLS + GLA_GATE_RANK, :].set(gwh)
    gw_hi = gw.astype(BF16)
    gw_lo = (gw - gw_hi.astype(F32)).astype(BF16)
    gb = gate_b.reshape(2, nh, dk).transpose(1, 0, 2).reshape(nh, 1, 2 * dk)
    in_specs = [
        pl.BlockSpec((1, seq, dk), lambda b, h: (b, 0, h)),
        pl.BlockSpec((1, seq, dk), lambda b, h: (b, 0, h)),
        pl.BlockSpec((1, seq, dv), lambda b, h: (b, 0, h)),
        pl.BlockSpec((1, seq, SMALL_W), lambda b, h: (b, 0, 0)),
        pl.BlockSpec((1, seq_c, dk), lambda b, h: (b, 0, h)),
        pl.BlockSpec((1, seq_c, dv), lambda b, h: (b, 0, h)),
        pl.BlockSpec((1, seq_c, SMALL_W), lambda b, h: (b, 0, 0)),
        pl.BlockSpec((1, SMALL_W, 2 * dk), lambda b, h: (h, 0, 0)),
        pl.BlockSpec((1, SMALL_W, 2 * dk), lambda b, h: (h, 0, 0)),
        pl.BlockSpec((1, 1, 2 * dk), lambda b, h: (h, 0, 0)),
        _const_spec((blk, blk)), _const_spec((CHUNK, CHUNK)), _const_spec((CHUNK, CHUNK)),
    ]
    scratch = [
        pltpu.VMEM((seq_t, 2 * dk), F32),
        pltpu.VMEM((seq_t, 2 * dk), F32),
        pltpu.VMEM((nct, dv, 2 * dk), BF16),
        pltpu.VMEM((nct, 8, 2 * dk), F32),
        pltpu.VMEM((nct, dv, 2 * dk), BF16),
        pltpu.VMEM((dv, 2 * dk), F32),
    ]
    return pl.pallas_call(
        functools.partial(_gla_kernel, seq, seq_c, blk),
        out_shape=jax.ShapeDtypeStruct((bsz, seq, nh * dv), BF16),
        grid=(bsz, nh),
        in_specs=in_specs,
        out_specs=pl.BlockSpec((1, seq, dv), lambda b, h: (b, 0, h)),
        scratch_shapes=scratch,
        compiler_params=_cparams(2),
        name="gla",
    )(q_l, k_l, v_l, sm_l, k_c, v_c, sm_c, gw_hi, gw_lo, gb,
      _block_tri(blk), jnp.asarray(tril), jnp.asarray(tril.T))


def _merge_up_kernel(y_ref, z_ref, o_ref, r_ref, gt_ref, x_ref, g1_ref, sh2_ref, sc2_ref,
                     sg_ref, gg_ref, n2_ref, wbs_ref, wbg_ref, wo_ref, wup_ref, cw_ref,
                     x1_ref, c0_ref, c1_ref, c2_ref):
    sub = math.gcd(y_ref.shape[1], SUB_ROWS)
    n_up = wup_ref.shape[1]
    col = lax.broadcasted_iota(jnp.int32, (GRID_W, UP_CHUNK), 0)
    first_col, last_col = col == 0, col == GRID_W - 1
    c_refs = (c0_ref, c1_ref, c2_ref)
    for s in range(y_ref.shape[1] // sub):
        rows = slice(s * sub, (s + 1) * sub)
        y = y_ref[0, rows, :] * _silu(z_ref[0, rows, :].astype(F32))
        y = y * lax.rsqrt(jnp.mean(y * y, axis=-1, keepdims=True) + EPS) * sg_ref[...]
        ys = _dot(y.astype(BF16), wbs_ref[...])
        heads = []
        for h in range(GLA_HEADS):
            oh = o_ref[0, rows, h * GLA_DV:(h + 1) * GLA_DV].astype(F32)
            heads.append(oh * lax.rsqrt(jnp.mean(oh * oh, axis=-1, keepdims=True) + EPS) * gg_ref[...])
        o = jnp.concatenate(heads, axis=1) * _silu(r_ref[0, rows, :].astype(F32))
        os_ = _dot(o.astype(BF16), wbg_ref[...])
        gt = gt_ref[0, rows, :].astype(F32)
        m = gt[:, :D_MODEL] * ys + gt[:, D_MODEL:] * os_
        out = _dot(m.astype(BF16), wo_ref[...])
        x1 = x_ref[0, rows, :] + g1_ref[0] * out
        x1_ref[0, rows, :] = x1
        h2 = x1 * lax.rsqrt(jnp.mean(x1 * x1, axis=-1, keepdims=True) + EPS) * n2_ref[...]
        h2 = (h2 * (1.0 + sc2_ref[0]) + sh2_ref[0]).astype(BF16)
        for cbk in range(n_up // UP_CHUNK):
            cols = slice(cbk * UP_CHUNK, (cbk + 1) * UP_CHUNK)
            u = _dot(h2, wup_ref[:, cols])
            for g in range(sub // GRID_W):
                ug = u[g * GRID_W:(g + 1) * GRID_W]
                grow = slice(s * sub + g * GRID_W, s * sub + (g + 1) * GRID_W)
                view = (GRID_W // PACK_ROWS, PACK_ROWS, UP_CHUNK)
                taps = (jnp.where(first_col, 0.0, pltpu.roll(ug, 1, 0)).astype(BF16).reshape(view),
                        ug.astype(BF16).reshape(view),
                        jnp.where(last_col, 0.0, pltpu.roll(ug, GRID_W - 1, 0)).astype(BF16).reshape(view))
                for dr in range(3):
                    acc = cw_ref[3 * dr, :, cols][None] * taps[0]
                    acc = acc + cw_ref[3 * dr + 1, :, cols][None] * taps[1]
                    acc = acc + cw_ref[3 * dr + 2, :, cols][None] * taps[2]
                    c_refs[dr][0, grow, cols] = acc.reshape(GRID_W, UP_CHUNK)


def _merge_up(y_raw, z, o, r, gates, x, gate1, shift2, scale2, ssd_g, gla_g, norm2_g,
              w_bs, w_bg, w_o, w_up, conv_w, tm):
    bsz, seq, _ = x.shape
    n_up = w_up.shape[1]
    assert tm % GRID_W == 0 and n_up % UP_CHUNK == 0
    conv_w = jnp.broadcast_to(conv_w.astype(BF16)[:, None, :], (conv_w.shape[0], PACK_ROWS, n_up))
    tok = lambda w: pl.BlockSpec((1, tm, w), lambda b, i: (b, i, 0))
    mod = pl.BlockSpec((1, 1, D_MODEL), lambda b, i: (b, 0, 0))
    return pl.pallas_call(
        _merge_up_kernel,
        out_shape=[jax.ShapeDtypeStruct((bsz, seq, D_MODEL), F32)]
        + [jax.ShapeDtypeStruct((bsz, seq, n_up), BF16)] * 3,
        grid=(bsz, seq // tm),
        in_specs=[tok(D_MODEL), tok(D_MODEL), tok(D_MODEL), tok(D_MODEL), tok(2 * D_MODEL), tok(D_MODEL),
                  mod, mod, mod, _const_spec((1, D_MODEL)), _const_spec((1, GLA_DV)),
                  _const_spec((1, D_MODEL)),
                  _const_spec(w_bs.shape), _const_spec(w_bg.shape), _const_spec(w_o.shape),
                  _const_spec(w_up.shape), _const_spec(conv_w.shape)],
        out_specs=[tok(D_MODEL), tok(n_up), tok(n_up), tok(n_up)],
        compiler_params=_cparams(2),
        name="merge_up",
    )(y_raw, z, o, r, gates, x, gate1, shift2, scale2, ssd_g, gla_g, norm2_g, w_bs, w_bg, w_o, w_up,
      conv_w)


FFN_TILE_ROWS = 4
FFN_CBLK = 256


def _ffn_down_kernel(n_row_tiles,
                     c0m_ref, c0t_ref, c1m_ref, c2m_ref, c2b_ref, x1_ref, g2_ref, cb_ref, wd_ref, fg_ref,
                     o_ref,
                     act_s):
    i = pl.program_id(1)
    tr, gw, cb = FFN_TILE_ROWS, GRID_W, FFN_CBLK
    top_on = jnp.where(i > 0, 1.0, 0.0)
    bot_on = jnp.where(i < n_row_tiles - 1, 1.0, 0.0)

    def conv_rows(rr, cols):
        rows = slice(rr * gw, (rr + 1) * gw)
        if rr == 0:
            top = c0t_ref[0, :, cols].astype(F32) * top_on
        else:
            top = c0m_ref[0, (rr - 1) * gw:rr * gw, cols].astype(F32)
        if rr == tr - 1:
            bot = c2b_ref[0, :, cols].astype(F32) * bot_on
        else:
            bot = c2m_ref[0, (rr + 1) * gw:(rr + 2) * gw, cols].astype(F32)
        return top + c1m_ref[0, rows, cols].astype(F32) + bot + cb_ref[0:1, cols]

    for jb in range(D_FF // cb):
        for rr in range(tr):
            gate = conv_rows(rr, slice(jb * cb, (jb + 1) * cb))
            val = conv_rows(rr, slice(D_FF + jb * cb, D_FF + (jb + 1) * cb))
            act_s[rr * gw:(rr + 1) * gw, jb * cb:(jb + 1) * cb] = (_silu(gate) * val).astype(BF16)

    ffn = _dot(act_s[...], wd_ref[...])
    x2 = x1_ref[0] + g2_ref[0] * ffn
    o_ref[0] = x2 * lax.rsqrt(jnp.mean(x2 * x2, axis=-1, keepdims=True) + EPS) * fg_ref[...]


def _ffn_down(c0, c1, c2, x1, gate2, conv_b, w_down, final_g):
    bsz, seq, n_up = c1.shape
    n_rows = seq // GRID_W
    tr = FFN_TILE_ROWS
    n_tiles = n_rows // tr
    tm = tr * GRID_W
    main = pl.BlockSpec((1, tm, n_up), lambda b, i: (b, i, 0))
    return pl.pallas_call(
        functools.partial(_ffn_down_kernel, n_tiles),
        out_shape=jax.ShapeDtypeStruct((bsz, seq, D_MODEL), F32),
        grid=(bsz, n_tiles),
        in_specs=[
            main,
            pl.BlockSpec((1, GRID_W, n_up), lambda b, i: (b, jnp.maximum(i * tr - 1, 0), 0)),
            main,
            main,
            pl.BlockSpec((1, GRID_W, n_up), lambda b, i: (b, jnp.minimum((i + 1) * tr, n_rows - 1), 0)),
            pl.BlockSpec((1, tm, D_MODEL), lambda b, i: (b, i, 0)),
            pl.BlockSpec((1, 1, D_MODEL), lambda b, i: (b, 0, 0)),
            _const_spec(conv_b.shape),
            _const_spec(w_down.shape), _const_spec((1, D_MODEL)),
        ],
        out_specs=pl.BlockSpec((1, tm, D_MODEL), lambda b, i: (b, i, 0)),
        scratch_shapes=[pltpu.VMEM((tm, D_FF), BF16)],
        compiler_params=_cparams(2),
        name="ffn_down",
    )(c0, c0, c1, c2, c2, x1, gate2, conv_b, w_down, final_g)


def kernel(x, c, ctx, c_ctx, w_ada, b_ada, norm1_g, w_in, ssd_conv_w, ssd_conv_b, ssd_dt_bias,
           ssd_a_log, ssd_d, ssd_norm_g, gla_gate_w, gla_gate_b, gla_norm_g, w_br_ssd, w_br_gla,
           w_merge, b_merge, w_o, norm2_g, w_up, ffn_conv_w, ffn_conv_b, w_down, final_norm_g):
    bsz, seq, _ = x.shape
    assert w_ada.shape[0] == 1, "single-layer trunk"
    row = lambda v: v.reshape(1, -1)

    n_cond = bsz + 1
    pad = (-n_cond) % 8
    cond = jnp.concatenate([c, c_ctx[None], jnp.zeros((pad, D_MODEL), F32)], axis=0)
    mod = _adaln(cond, w_ada[0], row(b_ada[0]))
    mx = mod[:bsz].reshape(bsz, N_MOD, 1, D_MODEL)
    mc = mod[bsz:bsz + 1].reshape(1, N_MOD, 1, D_MODEL)

    wi = w_in[0]
    o_z, o_xbc = 0, SSD_D_INNER
    o_dt = o_xbc + SSD_D_INNER + 2 * SSD_GROUPS * SSD_STATE
    o_q = o_dt + DT_COLS
    o_k = o_q + GLA_KEY_DIM
    o_v = o_k + GLA_KEY_DIM
    o_g = o_v + D_MODEL
    o_r = o_g + GLA_GATE_RANK
    w_small = jnp.zeros((D_MODEL, SMALL_W), F32)
    w_small = w_small.at[:, :DT_COLS].set(wi[:, o_dt:o_q])
    w_small = w_small.at[:, DT_COLS:DT_COLS + GLA_GATE_RANK].set(wi[:, o_g:o_r])
    bf = lambda w: w.astype(BF16)
    w_z, w_xbc = bf(wi[:, o_z:o_xbc]), bf(wi[:, o_xbc:o_dt])
    w_q, w_k, w_v, w_r = bf(wi[:, o_q:o_k]), bf(wi[:, o_k:o_v]), bf(wi[:, o_v:o_g]), bf(wi[:, o_r:])
    w_small = bf(w_small)
    bm = row(b_merge[0])
    g1 = row(norm1_g[0])

    xbc_c, k_c, v_c, sm_c = _inproj(
        ctx, mc[:, 0], mc[:, 1], g1, bm, [w_xbc, w_k, w_v, w_small],
        ["none"] * 4, [BF16, BF16, BF16, F32], tm=ctx.shape[1])
    z, xbc, q, k, v, r, gates, sm = _inproj(
        x, mx[:, 0], mx[:, 1], g1, bm,
        [w_z, w_xbc, w_q, w_k, w_v, w_r, bf(w_merge[0]), w_small],
        ["none"] * 6 + ["sigmoid_bias", "none"],
        [BF16] * 7 + [F32], tm=512)

    y_raw = _ssd(xbc, sm, xbc_c, sm_c, ssd_conv_w[0], row(ssd_conv_b[0]),
                 ssd_dt_bias[0], ssd_a_log[0], ssd_d[0])
    o = _gla(q, k, v, sm, k_c, v_c, sm_c, gla_gate_w[0], gla_gate_b[0])

    x1, c0, c1, c2 = _merge_up(y_raw, z, o, r, gates, x, mx[:, 2], mx[:, 3], mx[:, 4],
                               row(ssd_norm_g[0]), row(gla_norm_g[0]), row(norm2_g[0]),
                               bf(w_br_ssd[0]), bf(w_br_gla[0]), bf(w_o[0]), bf(w_up[0]),
                               ffn_conv_w[0].reshape(9, -1), tm=FFN_TILE_ROWS * GRID_W)
    return _ffn_down(c0, c1, c2, x1, mx[:, 5], row(ffn_conv_b[0]), bf(w_down[0]), row(final_norm_g))
```

```python
import functools
import math

import numpy as np
import jax
import jax.numpy as jnp
from jax import lax
from jax.experimental import pallas as pl
from jax.experimental.pallas import tpu as pltpu

F32 = jnp.float32
BF16 = jnp.bfloat16

D_MODEL = 1024
GRID_W = 64
CHUNK = 64
EPS = 1e-6
N_MOD = 6

SSD_HEAD_DIM = 64
SSD_HEADS = 16
SSD_GROUPS = 4
SSD_STATE = 128
SSD_GROUP_W = (SSD_HEADS // SSD_GROUPS) * SSD_HEAD_DIM
SSD_D_INNER = D_MODEL

GLA_HEADS = 4
GLA_DK = 128
GLA_DV = 256
GLA_GATE_RANK = 16
GLA_GATE_NORM = 16.0
GLA_KEY_DIM = GLA_HEADS * GLA_DK

D_FF = 2816
SMALL_W = 128
DT_COLS = 2 * SSD_HEADS
NEG_BIG = -1e30
MXU_DIM = 256
SUB_ROWS = 256
UP_CHUNK = 512
SCAN_UNROLL = 8
BLOCK_UNROLL = 4

VMEM_LIMIT_BYTES = 56 * 1024 * 1024


def _cparams(n_grid):
    return pltpu.CompilerParams(
        dimension_semantics=("arbitrary",) * n_grid,
        vmem_limit_bytes=VMEM_LIMIT_BYTES,
    )


def _const_spec(shape):
    nd = len(shape)
    return pl.BlockSpec(tuple(shape), lambda *_: (0,) * nd, pipeline_mode=pl.Buffered(1))


def _silu(v):
    return v * jax.nn.sigmoid(v)


def _softplus(v):
    return jnp.maximum(v, 0.0) + jnp.log1p(jnp.exp(-jnp.abs(v)))


def _split2(v):
    hi = v.astype(BF16)
    lo = (v - hi.astype(F32)).astype(BF16)
    return hi, lo


def _dot(a, b):
    return jnp.dot(a, b, preferred_element_type=F32)


def _dot_tn(a, b):
    return lax.dot_general(a, b, (((0,), (0,)), ((), ())), preferred_element_type=F32)


def _dot_nt(a, b):
    return lax.dot_general(a, b, (((1,), (1,)), ((), ())), preferred_element_type=F32)


def _scan_block_rows(seq, seq_c):
    return math.gcd(math.gcd(seq, seq_c), MXU_DIM)


def _block_tri(rows):
    idx = np.arange(rows)
    same = (idx[:, None] // CHUNK) == (idx[None, :] // CHUNK)
    return jnp.asarray((same & (idx[:, None] >= idx[None, :])).astype(np.float32), BF16)


def _adaln_kernel(c_ref, w_ref, b_ref, o_ref):
    s_hi, s_lo = _split2(_silu(c_ref[...]))
    w_hi, w_lo = _split2(w_ref[...])
    o_ref[...] = _dot(s_hi, w_hi) + _dot(s_lo, w_hi) + _dot(s_hi, w_lo) + b_ref[...]


def _adaln(cond, w, b):
    rows = cond.shape[0]
    n_out = w.shape[1]
    tn = D_MODEL
    return pl.pallas_call(
        _adaln_kernel,
        out_shape=jax.ShapeDtypeStruct((rows, n_out), F32),
        grid=(n_out // tn,),
        in_specs=[
            pl.BlockSpec((rows, D_MODEL), lambda j: (0, 0)),
            pl.BlockSpec((D_MODEL, tn), lambda j: (0, j)),
            pl.BlockSpec((1, tn), lambda j: (0, j)),
        ],
        out_specs=pl.BlockSpec((rows, tn), lambda j: (0, j)),
        compiler_params=_cparams(1),
        name="adaln",
    )(cond, w, b)


HALO_ROWS = 8
EPI_CHUNK = 256


def _inproj_kernel(plain_posts, n_tiles,
                   x_ref, xprev_ref, xnext_ref, shift_ref, scale_ref, g_ref, bias_ref,
                   wxbc_ref, cw_ref, cb_ref, wsm_ref, dtb_ref, gwh_ref, gb_ref,
                   *refs):
    n = len(plain_posts)
    w_refs = refs[:n]
    xbc_o, dt_o, la_o = refs[n:n + 3]
    o_refs = refs[n + 3:]
    i = pl.program_id(1)
    tm = x_ref.shape[1]
    sub = math.gcd(tm, SUB_ROWS)
    n_sub = tm // sub
    hr = HALO_ROWS

    def normmod(xv):
        ms = jnp.mean(xv * xv, axis=-1, keepdims=True)
        return ((xv * lax.rsqrt(ms + EPS)) * g_ref[...]) * (1.0 + scale_ref[0]) + shift_ref[0]

    for s in range(n_sub):
        rows = slice(s * sub, (s + 1) * sub)
        h = normmod(x_ref[0, rows, :])
        hb = h.astype(BF16)

        if s > 0:
            h_prev = normmod(x_ref[0, s * sub - hr:s * sub, :])
        else:
            h_prev = normmod(xprev_ref[0]) * jnp.where(i > 0, 1.0, 0.0)
        if s < n_sub - 1:
            h_next = normmod(x_ref[0, (s + 1) * sub:(s + 1) * sub + hr, :])
        else:
            h_next = normmod(xnext_ref[0]) * jnp.where(i < n_tiles - 1, 1.0, 0.0)
        h_ext = jnp.concatenate([h_prev, h, h_next], axis=0).astype(BF16)
        n_ext = sub + 2 * hr
        view = (sub // 8, 8, EPI_CHUNK)

        def xbc_chunk(cbk):
            cols = slice(cbk * EPI_CHUNK, (cbk + 1) * EPI_CHUNK)
            xe = _dot(h_ext, wxbc_ref[:, cols])
            cur = xe[hr:hr + sub].reshape(view)
            xm1 = pltpu.roll(xe, 1, 0)[hr:hr + sub].reshape(view)
            xp1 = pltpu.roll(xe, n_ext - 1, 0)[hr:hr + sub].reshape(view)
            yv = (cw_ref[0, :, cols][None] * xm1 + cw_ref[1, :, cols][None] * cur
                  + cw_ref[2, :, cols][None] * xp1 + cb_ref[:, cols][None])
            xbc_o[0, rows, cols] = _silu(yv).reshape(sub, EPI_CHUNK).astype(xbc_o.dtype)

        small = _dot(hb, wsm_ref[...])
        dt_o[0, rows, :] = _softplus(small + dtb_ref[...])
        s_hi, s_lo = _split2(small)

        def gate_chunk(cbk):
            cols = slice(cbk * EPI_CHUNK, (cbk + 1) * EPI_CHUNK)
            logit = _dot(s_hi, gwh_ref[:, cols]) + _dot(s_lo, gwh_ref[:, cols]) + gb_ref[:, cols]
            la_o[0, rows, cols] = ((jnp.minimum(logit, 0.0) - jnp.log1p(jnp.exp(-jnp.abs(logit))))
                                   * (1.0 / GLA_GATE_NORM))

        def plain(k):
            acc = _dot(hb, w_refs[k][...])
            if plain_posts[k] == "sigmoid_bias":
                acc = jax.nn.sigmoid(acc + bias_ref[...])
            o_refs[k][0, rows, :] = acc.astype(o_refs[k].dtype)

        epi = ([functools.partial(xbc_chunk, c) for c in range(wxbc_ref.shape[1] // EPI_CHUNK)]
               + [functools.partial(gate_chunk, c) for c in range(gwh_ref.shape[1] // EPI_CHUNK)])
        plains = [functools.partial(plain, k) for k in range(n)]
        for t in range(max(len(epi), len(plains))):
            if t < len(epi):
                epi[t]()
            if t < len(plains):
                plains[t]()


def _inproj(x, shift, scale, g, bias, w_xbc, conv_w, conv_b, w_small, dtb, gw_hi, gb,
            weights, posts, out_dtypes, tm):
    bsz, seq, _ = x.shape
    n_tiles = seq // tm
    hr = HALO_ROWS
    assert tm % hr == 0
    tpb = tm // hr
    n_xbc, n_la = w_xbc.shape[1], gw_hi.shape[1]
    mod_map = (lambda b, i: (b, 0, 0)) if shift.shape[0] > 1 else (lambda b, i: (0, 0, 0))
    conv_w = jnp.broadcast_to(conv_w[:, None, :], (conv_w.shape[0], 8, n_xbc))
    conv_b = jnp.broadcast_to(conv_b, (8, n_xbc))
    in_specs = [
        pl.BlockSpec((1, tm, D_MODEL), lambda b, i: (b, i, 0)),
        pl.BlockSpec((1, hr, D_MODEL), lambda b, i: (b, jnp.maximum(i * tpb - 1, 0), 0)),
        pl.BlockSpec((1, hr, D_MODEL), lambda b, i: (b, jnp.minimum((i + 1) * tpb, seq // hr - 1), 0)),
        pl.BlockSpec((1, 1, D_MODEL), mod_map),
        pl.BlockSpec((1, 1, D_MODEL), mod_map),
        _const_spec((1, D_MODEL)),
        _const_spec(bias.shape),
        _const_spec(w_xbc.shape), _const_spec(conv_w.shape), _const_spec(conv_b.shape),
        _const_spec(w_small.shape), _const_spec(dtb.shape),
        _const_spec(gw_hi.shape), _const_spec(gb.shape),
    ] + [_const_spec(w.shape) for w in weights]
    tok = lambda w: pl.BlockSpec((1, tm, w), lambda b, i: (b, i, 0))
    out_shape = ([jax.ShapeDtypeStruct((bsz, seq, n_xbc), BF16),
                  jax.ShapeDtypeStruct((bsz, seq, SMALL_W), F32),
                  jax.ShapeDtypeStruct((bsz, seq, n_la), F32)]
                 + [jax.ShapeDtypeStruct((bsz, seq, w.shape[1]), dt) for w, dt in zip(weights, out_dtypes)])
    out_specs = [tok(n_xbc), tok(SMALL_W), tok(n_la)] + [tok(w.shape[1]) for w in weights]
    return pl.pallas_call(
        functools.partial(_inproj_kernel, tuple(posts), n_tiles),
        out_shape=out_shape,
        grid=(bsz, n_tiles),
        in_specs=in_specs,
        out_specs=out_specs,
        compiler_params=_cparams(2),
        name="inproj",
    )(x, x, x, shift, scale, g, bias, w_xbc, conv_w, conv_b, w_small, dtb, gw_hi, gb, *weights)


PACK_ROWS = 16


def _bwd_chunk_order(i, nc_c, nct):
    return jnp.where(i < nc_c, nc_c - 1 - i, nct - 1 - (i - nc_c))


def _ssd_kernel(seq, seq_c, blk,
                xs_l, bm_l, cm_l, dt_l, xs_c, bm_c, dt_c,
                e_ref, alog_ref, dskip_ref,
                tri_ref, itile_ref, maskf_ref, maskb_ref, bd_ref,
                y_ref,
                dtf_s, dtb_s, cumf_s, cumb_s,
                sf_s, sb_s, ef_s, eb_s, hf_s, hb_s, hrun_f, hrun_b):
    nc, nc_c = seq // CHUNK, seq_c // CHUNK
    nct = nc + nc_c
    seq_t = seq + seq_c
    gw = SSD_GROUP_W
    cpb = blk // CHUNK

    def expand_rows(dt_ref, dst_off, n_rows):
        def body(i, carry):
            r0 = pl.multiple_of(i * blk, blk)
            dst = pl.ds(pl.multiple_of(dst_off + r0, CHUNK), blk)
            hi, lo = _split2(dt_ref[0, pl.ds(r0, blk), :])
            dtf_s[dst, :] = _dot(hi, e_ref[0, 0]) + _dot(lo, e_ref[0, 0])
            dtb_s[dst, :] = _dot(hi, e_ref[0, 1]) + _dot(lo, e_ref[0, 1])
            return carry
        lax.fori_loop(0, n_rows // blk, body, 0, unroll=BLOCK_UNROLL)

    expand_rows(dt_c, 0, seq_c)
    expand_rows(dt_l, seq_c, seq)

    a_f = -jnp.exp(alog_ref[0, 0:1, :])
    a_b = -jnp.exp(alog_ref[0, 1:2, :])
    tri = tri_ref[...]

    def cumsums(i, carry):
        r0 = pl.multiple_of(i * blk, blk)
        rows = pl.ds(r0, blk)
        da_b = dtb_s[rows, :] * a_b
        hi, lo = _split2(jnp.concatenate([dtf_s[rows, :] * a_f, da_b], axis=1))
        pre = _dot(tri, hi) + _dot(tri, lo)
        cumf_s[rows, :] = pre[:, :gw]
        for c in range(cpb):
            lo_r, hi_r = c * CHUNK, (c + 1) * CHUNK
            tot = pre[hi_r - 1:hi_r, gw:]
            cumb_s[pl.ds(r0 + lo_r, CHUNK), :] = tot - pre[lo_r:hi_r, gw:] + da_b[lo_r:hi_r, :]
        return carry

    lax.fori_loop(0, seq_t // blk, cumsums, 0, unroll=BLOCK_UNROLL)

    def state_rows(xs_ref, bm_ref, chunk_off, n_chunks):
        def body(j, carry):
            rows = pl.ds(pl.multiple_of(j * CHUNK, CHUNK), CHUNK)
            trow = pl.ds(pl.multiple_of((chunk_off + j) * CHUNK, CHUNK), CHUNK)
            cum_f, cum_b = cumf_s[trow, :], cumb_s[trow, :]
            last_f = cum_f[CHUNK - 1:CHUNK, :]
            tot_b = cum_b[0:1, :]
            xs = xs_ref[0, rows, :].astype(F32)
            w_f = (xs * (dtf_s[trow, :] * jnp.exp(last_f - cum_f))).astype(BF16)
            w_b = (xs * (dtb_s[trow, :] * jnp.exp(tot_b - cum_b))).astype(BF16)
            st = _dot_tn(bm_ref[0, rows, :], jnp.concatenate([w_f, w_b], axis=1))
            sf_s[chunk_off + j] = st[:, :gw].astype(BF16)
            sb_s[chunk_off + j] = st[:, gw:].astype(BF16)
            ef_s[chunk_off + j] = jnp.broadcast_to(jnp.exp(last_f), (8, gw))
            eb_s[chunk_off + j] = jnp.broadcast_to(jnp.exp(tot_b), (8, gw))
            return carry
        lax.fori_loop(0, n_chunks, body, 0, unroll=SCAN_UNROLL)

    state_rows(xs_c, bm_c, 0, nc_c)
    state_rows(xs_l, bm_l, nc_c, nc)

    hrun_f[...] = jnp.zeros_like(hrun_f)
    hrun_b[...] = jnp.zeros_like(hrun_b)

    def recur(i, carry):
        h = hrun_f[...]
        hf_s[i] = h.astype(BF16)
        hrun_f[...] = ef_s[i][0:1, :] * h + sf_s[i].astype(F32)
        jb = _bwd_chunk_order(i, nc_c, nct)
        g = hrun_b[...]
        hb_s[jb] = g.astype(BF16)
        hrun_b[...] = eb_s[jb][0:1, :] * g + sb_s[jb].astype(F32)
        return carry

    lax.fori_loop(0, nct, recur, 0, unroll=BLOCK_UNROLL)

    itile = itile_ref[...]
    dskip = dskip_ref[0]

    def rowvec(v):
        return jnp.sum(v * itile, axis=0, keepdims=True)

    def lat_out(j, carry):
        r0 = pl.multiple_of(j * CHUNK, CHUNK)
        rows = pl.ds(r0, CHUNK)
        trow = pl.ds(pl.multiple_of(seq_c + r0, CHUNK), CHUNK)
        xb = xs_l[0, rows, :]
        bc = bm_l[0, rows, :]
        cc = cm_l[0, rows, :]
        cb4 = _dot_nt(cc, jnp.concatenate([bc, bc, bc, bc], axis=0))
        cum_f, cum_b = cumf_s[trow, :], cumb_s[trow, :]
        lf = jnp.exp(cum_f - rowvec(cum_f) + maskf_ref[...]) * rowvec(dtf_s[trow, :])
        lb = jnp.exp(cum_b - rowvec(cum_b) + maskb_ref[...]) * rowvec(dtb_s[trow, :])
        m = (cb4 * (lf + lb)).astype(BF16)
        xbd = jnp.concatenate([xb, xb, xb, xb], axis=0) * bd_ref[...]
        y = _dot(m, xbd)
        y = y + jnp.exp(cum_f) * _dot(cc, hf_s[nc_c + j]) + jnp.exp(cum_b) * _dot(cc, hb_s[nc_c + j])
        y_ref[0, rows, :] = y + dskip * xb.astype(F32)
        return carry

    lax.fori_loop(0, nc, lat_out, 0, unroll=SCAN_UNROLL)


def _ssd_consts():
    q = CHUNK
    idx = np.arange(q)
    tri = (idx[:, None] >= idx[None, :]).astype(np.float32)
    itile = np.tile(np.eye(q, dtype=np.float32), (1, SSD_GROUP_W // q))
    maskf = np.tile((1.0 - tri) * NEG_BIG, (1, SSD_GROUP_W // q)).astype(np.float32)
    maskb = np.tile((1.0 - tri.T) * NEG_BIG, (1, SSD_GROUP_W // q)).astype(np.float32)
    blk = np.arange(SSD_GROUP_W) // SSD_HEAD_DIM
    bd = (blk[:, None] == blk[None, :]).astype(np.float32)
    e = np.zeros((SSD_GROUPS, 2, SMALL_W, SSD_GROUP_W), np.float32)
    for g in range(SSD_GROUPS):
        for d in range(2):
            for r in range(SSD_GROUPS):
                e[g, d, d * SSD_HEADS + g * 4 + r, r * SSD_HEAD_DIM:(r + 1) * SSD_HEAD_DIM] = 1.0
    return (jnp.asarray(itile), jnp.asarray(maskf), jnp.asarray(maskb),
            jnp.asarray(bd, BF16), jnp.asarray(e, BF16))


def _ssd(xbc_l, dt_l, xbc_c, dt_c, a_log, d_skip):
    bsz, seq, _ = xbc_l.shape
    seq_c = xbc_c.shape[1]
    seq_t = seq + seq_c
    nct = seq_t // CHUNK
    blk = _scan_block_rows(seq, seq_c)
    gw, ns, ng = SSD_GROUP_W, SSD_STATE, SSD_GROUPS
    itile, maskf, maskb, bd, e = _ssd_consts()
    tri = _block_tri(blk)
    alog = jnp.repeat(a_log.reshape(2, ng, 4), SSD_HEAD_DIM, axis=-1).transpose(1, 0, 2)
    dsk = jnp.repeat(d_skip.reshape(ng, 4), SSD_HEAD_DIM, axis=-1).reshape(ng, 1, gw)
    b_off = SSD_D_INNER // ns
    c_off = b_off + ng
    in_specs = [
        pl.BlockSpec((1, seq, gw), lambda b, g: (b, 0, g)),
        pl.BlockSpec((1, seq, ns), lambda b, g: (b, 0, b_off + g)),
        pl.BlockSpec((1, seq, ns), lambda b, g: (b, 0, c_off + g)),
        pl.BlockSpec((1, seq, SMALL_W), lambda b, g: (b, 0, 0)),
        pl.BlockSpec((1, seq_c, gw), lambda b, g: (b, 0, g)),
        pl.BlockSpec((1, seq_c, ns), lambda b, g: (b, 0, b_off + g)),
        pl.BlockSpec((1, seq_c, SMALL_W), lambda b, g: (b, 0, 0)),
        pl.BlockSpec((1, 2, SMALL_W, gw), lambda b, g: (g, 0, 0, 0)),
        pl.BlockSpec((1, 2, gw), lambda b, g: (g, 0, 0)),
        pl.BlockSpec((1, 1, gw), lambda b, g: (g, 0, 0)),
        _const_spec(tri.shape), _const_spec(itile.shape), _const_spec(maskf.shape),
        _const_spec(maskb.shape), _const_spec(bd.shape),
    ]
    scratch = [
        pltpu.VMEM((seq_t, gw), F32),
        pltpu.VMEM((seq_t, gw), F32),
        pltpu.VMEM((seq_t, gw), F32),
        pltpu.VMEM((seq_t, gw), F32),
        pltpu.VMEM((nct, ns, gw), BF16),
        pltpu.VMEM((nct, ns, gw), BF16),
        pltpu.VMEM((nct, 8, gw), F32),
        pltpu.VMEM((nct, 8, gw), F32),
        pltpu.VMEM((nct, ns, gw), BF16),
        pltpu.VMEM((nct, ns, gw), BF16),
        pltpu.VMEM((ns, gw), F32),
        pltpu.VMEM((ns, gw), F32),
    ]
    return pl.pallas_call(
        functools.partial(_ssd_kernel, seq, seq_c, blk),
        out_shape=jax.ShapeDtypeStruct((bsz, seq, SSD_D_INNER), F32),
        grid=(bsz, ng),
        in_specs=in_specs,
        out_specs=pl.BlockSpec((1, seq, gw), lambda b, g: (b, 0, g)),
        scratch_shapes=scratch,
        compiler_params=_cparams(2),
        name="ssd",
    )(xbc_l, xbc_l, xbc_l, dt_l, xbc_c, xbc_c, dt_c,
      e, alog, dsk, tri, itile, maskf, maskb, bd)


def _gla_kernel(seq, seq_c, blk,
                q_l, k_l, v_l, la_l, k_c, v_c, la_c,
                tri_ref, tril_ref, triu_ref,
                o_ref,
                cum_s, st_s, e_s, hs_s, hrun):
    nc, nc_c = seq // CHUNK, seq_c // CHUNK
    nct = nc + nc_c
    dk = GLA_DK
    cpb = blk // CHUNK
    tri = tri_ref[...]
    scale = GLA_DK ** -0.5
    mid = CHUNK // 2

    def cumsum_rows(la_ref, dst_off, n_rows):
        def body(i, carry):
            r0 = pl.multiple_of(i * blk, blk)
            d0 = pl.multiple_of(dst_off + r0, CHUNK)
            la = la_ref[0, pl.ds(r0, blk), :]
            lhi, llo = _split2(la)
            pre = _dot(tri, lhi) + _dot(tri, llo)
            cum_s[pl.ds(d0, blk), 0:dk] = pre[:, :dk]
            for c in range(cpb):
                lo_r, hi_r = c * CHUNK, (c + 1) * CHUNK
                tot = pre[hi_r - 1:hi_r, dk:]
                cum_s[pl.ds(d0 + lo_r, CHUNK), dk:2 * dk] = tot - pre[lo_r:hi_r, dk:] + la[lo_r:hi_r, dk:]
            return carry
        lax.fori_loop(0, n_rows // blk, body, 0, unroll=BLOCK_UNROLL)

    cumsum_rows(la_c, 0, seq_c)
    cumsum_rows(la_l, seq_c, seq)

    def state_rows(k_ref, v_ref, chunk_off, n_chunks):
        def body(j, carry):
            rows = pl.ds(pl.multiple_of(j * CHUNK, CHUNK), CHUNK)
            trow = pl.ds(pl.multiple_of((chunk_off + j) * CHUNK, CHUNK), CHUNK)
            cum = cum_s[trow, :]
            cum_f, cum_b = cum[:, :dk], cum[:, dk:]
            last_f = cum_f[CHUNK - 1:CHUNK, :]
            tot_b = cum_b[0:1, :]
            k = k_ref[0, rows, :].astype(F32)
            kd = jnp.concatenate([k * jnp.exp(last_f - cum_f), k * jnp.exp(tot_b - cum_b)], axis=1)
            st_s[chunk_off + j] = _dot_tn(v_ref[0, rows, :], kd.astype(BF16)).astype(BF16)
            e_s[chunk_off + j] = jnp.broadcast_to(
                jnp.concatenate([jnp.exp(last_f), jnp.exp(tot_b)], axis=1), (8, 2 * dk))
            return carry
        lax.fori_loop(0, n_chunks, body, 0, unroll=SCAN_UNROLL)

    state_rows(k_c, v_c, 0, nc_c)
    state_rows(k_l, v_l, nc_c, nc)

    hrun[...] = jnp.zeros_like(hrun)

    def recur(i, carry):
        jb = _bwd_chunk_order(i, nc_c, nct)
        h = hrun[...]
        hs_s[i, :, 0:dk] = h[:, :dk].astype(BF16)
        hs_s[jb, :, dk:2 * dk] = h[:, dk:].astype(BF16)
        dec = jnp.concatenate([e_s[i][0:1, :dk], e_s[jb][0:1, dk:]], axis=1)
        inc = jnp.concatenate([st_s[i, :, 0:dk], st_s[jb, :, dk:2 * dk]], axis=1).astype(F32)
        hrun[...] = dec * h + inc
        return carry

    lax.fori_loop(0, nct, recur, 0, unroll=BLOCK_UNROLL)

    def lat_out(j, carry):
        r0 = pl.multiple_of(j * CHUNK, CHUNK)
        rows = pl.ds(r0, CHUNK)
        trow = pl.ds(pl.multiple_of(seq_c + r0, CHUNK), CHUNK)
        cum = cum_s[trow, :]
        cum_f, cum_b = cum[:, :dk], cum[:, dk:]
        ref_f = cum_f[mid - 1:mid, :]
        ref_b = cum_b[mid:mid + 1, :]
        q = q_l[0, rows, :].astype(F32) * scale
        k = k_l[0, rows, :].astype(F32)
        v = v_l[0, rows, :]
        sc_f = _dot_nt((q * jnp.exp(cum_f - ref_f)).astype(BF16), (k * jnp.exp(ref_f - cum_f)).astype(BF16))
        sc_b = _dot_nt((q * jnp.exp(cum_b - ref_b)).astype(BF16), (k * jnp.exp(ref_b - cum_b)).astype(BF16))
        p = (sc_f * tril_ref[...] + sc_b * triu_ref[...]).astype(BF16)
        qe = jnp.concatenate([q * jnp.exp(cum_f), q * jnp.exp(cum_b)], axis=1).astype(BF16)
        o_ref[0, rows, :] = (_dot(p, v) + _dot_nt(qe, hs_s[nc_c + j])).astype(o_ref.dtype)
        return carry

    lax.fori_loop(0, nc, lat_out, 0, unroll=SCAN_UNROLL)


def _gla_gate_params(gate_w, gate_b):
    dk, nh = GLA_DK, GLA_HEADS
    gwh = gate_w.reshape(2, GLA_GATE_RANK, nh, dk).transpose(1, 2, 0, 3).reshape(GLA_GATE_RANK, nh * 2 * dk)
    gw = jnp.zeros((SMALL_W, nh * 2 * dk), F32).at[DT_COLS:DT_COLS + GLA_GATE_RANK, :].set(gwh)
    gb = gate_b.reshape(2, nh, dk).transpose(1, 0, 2).reshape(1, nh * 2 * dk)
    return gw.astype(BF16), gb


def _gla(q_l, k_l, v_l, la_l, k_c, v_c, la_c):
    bsz, seq, _ = q_l.shape
    seq_c = k_c.shape[1]
    seq_t = seq + seq_c
    nct = seq_t // CHUNK
    blk = _scan_block_rows(seq, seq_c)
    dk, dv, nh = GLA_DK, GLA_DV, GLA_HEADS
    idx = np.arange(CHUNK)
    tril = (idx[:, None] >= idx[None, :]).astype(np.float32)
    in_specs = [
        pl.BlockSpec((1, seq, dk), lambda b, h: (b, 0, h)),
        pl.BlockSpec((1, seq, dk), lambda b, h: (b, 0, h)),
        pl.BlockSpec((1, seq, dv), lambda b, h: (b, 0, h)),
        pl.BlockSpec((1, seq, 2 * dk), lambda b, h: (b, 0, h)),
        pl.BlockSpec((1, seq_c, dk), lambda b, h: (b, 0, h)),
        pl.BlockSpec((1, seq_c, dv), lambda b, h: (b, 0, h)),
        pl.BlockSpec((1, seq_c, 2 * dk), lambda b, h: (b, 0, h)),
        _const_spec((blk, blk)), _const_spec((CHUNK, CHUNK)), _const_spec((CHUNK, CHUNK)),
    ]
    scratch = [
        pltpu.VMEM((seq_t, 2 * dk), F32),
        pltpu.VMEM((nct, dv, 2 * dk), BF16),
        pltpu.VMEM((nct, 8, 2 * dk), F32),
        pltpu.VMEM((nct, dv, 2 * dk), BF16),
        pltpu.VMEM((dv, 2 * dk), F32),
    ]
    return pl.pallas_call(
        functools.partial(_gla_kernel, seq, seq_c, blk),
        out_shape=jax.ShapeDtypeStruct((bsz, seq, nh * dv), BF16),
        grid=(bsz, nh),
        in_specs=in_specs,
        out_specs=pl.BlockSpec((1, seq, dv), lambda b, h: (b, 0, h)),
        scratch_shapes=scratch,
        compiler_params=_cparams(2),
        name="gla",
    )(q_l, k_l, v_l, la_l, k_c, v_c, la_c,
      _block_tri(blk), jnp.asarray(tril), jnp.asarray(tril.T))


def _merge_up_kernel(y_ref, z_ref, o_ref, r_ref, gt_ref, x_ref, g1_ref, sh2_ref, sc2_ref,
                     sg_ref, gg_ref, n2_ref, wbs_ref, wbg_ref, wo_ref, wup_ref, cw_ref,
                     x1_ref, cs_ref, e0_ref, e2_ref):
    n_g = y_ref.shape[1] // GRID_W
    n_up = wup_ref.shape[1]
    col = lax.broadcasted_iota(jnp.int32, (GRID_W, UP_CHUNK), 0)
    first_col, last_col = col == 0, col == GRID_W - 1

    y = y_ref[0] * _silu(z_ref[0].astype(F32))
    y = y * lax.rsqrt(jnp.mean(y * y, axis=-1, keepdims=True) + EPS) * sg_ref[...]
    ys = _dot(y.astype(BF16), wbs_ref[...])
    heads = []
    for h in range(GLA_HEADS):
        oh = o_ref[0, :, h * GLA_DV:(h + 1) * GLA_DV].astype(F32)
        heads.append(oh * lax.rsqrt(jnp.mean(oh * oh, axis=-1, keepdims=True) + EPS) * gg_ref[...])
    o = jnp.concatenate(heads, axis=1) * _silu(r_ref[0].astype(F32))
    os_ = _dot(o.astype(BF16), wbg_ref[...])
    gt = gt_ref[0].astype(F32)
    m = gt[:, :D_MODEL] * ys + gt[:, D_MODEL:] * os_
    out = _dot(m.astype(BF16), wo_ref[...])
    x1 = x_ref[0] + g1_ref[0] * out
    x1_ref[0] = x1
    h2 = x1 * lax.rsqrt(jnp.mean(x1 * x1, axis=-1, keepdims=True) + EPS) * n2_ref[...]
    h2 = (h2 * (1.0 + sc2_ref[0]) + sh2_ref[0]).astype(BF16)

    view = (GRID_W // PACK_ROWS, PACK_ROWS, UP_CHUNK)
    for cbk in range(n_up // UP_CHUNK):
        cols = slice(cbk * UP_CHUNK, (cbk + 1) * UP_CHUNK)
        u = _dot(h2, wup_ref[:, cols])
        c = [[None] * n_g for _ in range(3)]
        for g in range(n_g):
            ug = u[g * GRID_W:(g + 1) * GRID_W]
            taps = (jnp.where(first_col, 0.0, pltpu.roll(ug, 1, 0)).astype(BF16).reshape(view),
                    ug.astype(BF16).reshape(view),
                    jnp.where(last_col, 0.0, pltpu.roll(ug, GRID_W - 1, 0)).astype(BF16).reshape(view))
            for dr in range(3):
                acc = cw_ref[3 * dr, :, cols][None] * taps[0]
                acc = acc + cw_ref[3 * dr + 1, :, cols][None] * taps[1]
                acc = acc + cw_ref[3 * dr + 2, :, cols][None] * taps[2]
                c[dr][g] = acc.reshape(GRID_W, UP_CHUNK)
        for g in range(n_g):
            part = c[1][g]
            if g > 0:
                part = part + c[0][g - 1]
            if g < n_g - 1:
                part = part + c[2][g + 1]
            cs_ref[0, g * GRID_W:(g + 1) * GRID_W, cols] = part
        e0_ref[0, :, cols] = c[0][n_g - 1]
        e2_ref[0, :, cols] = c[2][0]


def _merge_up(y_raw, z, o, r, gates, x, gate1, shift2, scale2, ssd_g, gla_g, norm2_g,
              w_bs, w_bg, w_o, w_up, conv_w, tm):
    bsz, seq, _ = x.shape
    n_up = w_up.shape[1]
    assert tm % GRID_W == 0 and n_up % UP_CHUNK == 0
    conv_w = jnp.broadcast_to(conv_w.astype(BF16)[:, None, :], (conv_w.shape[0], PACK_ROWS, n_up))
    tok = lambda w: pl.BlockSpec((1, tm, w), lambda b, i: (b, i, 0))
    mod = pl.BlockSpec((1, 1, D_MODEL), lambda b, i: (b, 0, 0))
    return pl.pallas_call(
        _merge_up_kernel,
        out_shape=[jax.ShapeDtypeStruct((bsz, seq, D_MODEL), F32),
                   jax.ShapeDtypeStruct((bsz, seq, n_up), BF16),
                   jax.ShapeDtypeStruct((bsz, (seq // tm) * GRID_W, n_up), BF16),
                   jax.ShapeDtypeStruct((bsz, (seq // tm) * GRID_W, n_up), BF16)],
        grid=(bsz, seq // tm),
        in_specs=[tok(D_MODEL), tok(D_MODEL), tok(D_MODEL), tok(D_MODEL), tok(2 * D_MODEL), tok(D_MODEL),
                  mod, mod, mod, _const_spec((1, D_MODEL)), _const_spec((1, GLA_DV)),
                  _const_spec((1, D_MODEL)),
                  _const_spec(w_bs.shape), _const_spec(w_bg.shape), _const_spec(w_o.shape),
                  _const_spec(w_up.shape), _const_spec(conv_w.shape)],
        out_specs=[tok(D_MODEL), tok(n_up),
                   pl.BlockSpec((1, GRID_W, n_up), lambda b, i: (b, i, 0)),
                   pl.BlockSpec((1, GRID_W, n_up), lambda b, i: (b, i, 0))],
        compiler_params=_cparams(2),
        name="merge_up",
    )(y_raw, z, o, r, gates, x, gate1, shift2, scale2, ssd_g, gla_g, norm2_g, w_bs, w_bg, w_o, w_up,
      conv_w)


FFN_TILE_ROWS = 4
FFN_CBLK = 256


def _ffn_down_kernel(n_row_tiles,
                     cs_ref, e0_ref, e2_ref, x1_ref, g2_ref, cb_ref, wd_ref, fg_ref,
                     o_ref,
                     act_s):
    i = pl.program_id(1)
    tr, gw, cb = FFN_TILE_ROWS, GRID_W, FFN_CBLK
    top_on = jnp.where(i > 0, 1.0, 0.0)
    bot_on = jnp.where(i < n_row_tiles - 1, 1.0, 0.0)

    def conv_rows(rr, cols):
        v = cs_ref[0, rr * gw:(rr + 1) * gw, cols].astype(F32) + cb_ref[0:1, cols]
        if rr == 0:
            v = v + e0_ref[0, :, cols].astype(F32) * top_on
        if rr == tr - 1:
            v = v + e2_ref[0, :, cols].astype(F32) * bot_on
        return v

    for jb in range(D_FF // cb):
        for rr in range(tr):
            gate = conv_rows(rr, slice(jb * cb, (jb + 1) * cb))
            val = conv_rows(rr, slice(D_FF + jb * cb, D_FF + (jb + 1) * cb))
            act_s[rr * gw:(rr + 1) * gw, jb * cb:(jb + 1) * cb] = (_silu(gate) * val).astype(BF16)

    ffn = _dot(act_s[...], wd_ref[...])
    x2 = x1_ref[0] + g2_ref[0] * ffn
    o_ref[0] = x2 * lax.rsqrt(jnp.mean(x2 * x2, axis=-1, keepdims=True) + EPS) * fg_ref[...]


def _ffn_down(cs, e0, e2, x1, gate2, conv_b, w_down, final_g):
    bsz, seq, n_up = cs.shape
    tr = FFN_TILE_ROWS
    tm = tr * GRID_W
    n_tiles = seq // tm
    return pl.pallas_call(
        functools.partial(_ffn_down_kernel, n_tiles),
        out_shape=jax.ShapeDtypeStruct((bsz, seq, D_MODEL), F32),
        grid=(bsz, n_tiles),
        in_specs=[
            pl.BlockSpec((1, tm, n_up), lambda b, i: (b, i, 0)),
            pl.BlockSpec((1, GRID_W, n_up), lambda b, i: (b, jnp.maximum(i - 1, 0), 0)),
            pl.BlockSpec((1, GRID_W, n_up), lambda b, i: (b, jnp.minimum(i + 1, n_tiles - 1), 0)),
            pl.BlockSpec((1, tm, D_MODEL), lambda b, i: (b, i, 0)),
            pl.BlockSpec((1, 1, D_MODEL), lambda b, i: (b, 0, 0)),
            _const_spec(conv_b.shape),
            _const_spec(w_down.shape), _const_spec((1, D_MODEL)),
        ],
        out_specs=pl.BlockSpec((1, tm, D_MODEL), lambda b, i: (b, i, 0)),
        scratch_shapes=[pltpu.VMEM((tm, D_FF), BF16)],
        compiler_params=_cparams(2),
        name="ffn_down",
    )(cs, e0, e2, x1, gate2, conv_b, w_down, final_g)


def kernel(x, c, ctx, c_ctx, w_ada, b_ada, norm1_g, w_in, ssd_conv_w, ssd_conv_b, ssd_dt_bias,
           ssd_a_log, ssd_d, ssd_norm_g, gla_gate_w, gla_gate_b, gla_norm_g, w_br_ssd, w_br_gla,
           w_merge, b_merge, w_o, norm2_g, w_up, ffn_conv_w, ffn_conv_b, w_down, final_norm_g):
    bsz, seq, _ = x.shape
    assert w_ada.shape[0] == 1, "single-layer trunk"
    row = lambda v: v.reshape(1, -1)

    n_cond = bsz + 1
    pad = (-n_cond) % 8
    cond = jnp.concatenate([c, c_ctx[None], jnp.zeros((pad, D_MODEL), F32)], axis=0)
    mod = _adaln(cond, w_ada[0], row(b_ada[0]))
    mx = mod[:bsz].reshape(bsz, N_MOD, 1, D_MODEL)
    mc = mod[bsz:bsz + 1].reshape(1, N_MOD, 1, D_MODEL)

    wi = w_in[0]
    o_z, o_xbc = 0, SSD_D_INNER
    o_dt = o_xbc + SSD_D_INNER + 2 * SSD_GROUPS * SSD_STATE
    o_q = o_dt + DT_COLS
    o_k = o_q + GLA_KEY_DIM
    o_v = o_k + GLA_KEY_DIM
    o_g = o_v + D_MODEL
    o_r = o_g + GLA_GATE_RANK
    w_small = jnp.zeros((D_MODEL, SMALL_W), F32)
    w_small = w_small.at[:, :DT_COLS].set(wi[:, o_dt:o_q])
    w_small = w_small.at[:, DT_COLS:DT_COLS + GLA_GATE_RANK].set(wi[:, o_g:o_r])
    bf = lambda w: w.astype(BF16)
    w_z, w_xbc = bf(wi[:, o_z:o_xbc]), bf(wi[:, o_xbc:o_dt])
    w_q, w_k, w_v, w_r = bf(wi[:, o_q:o_k]), bf(wi[:, o_k:o_v]), bf(wi[:, o_v:o_g]), bf(wi[:, o_r:])
    w_small = bf(w_small)
    bm = row(b_merge[0])
    g1 = row(norm1_g[0])

    dtb = jnp.zeros((1, SMALL_W), F32).at[0, :DT_COLS].set(ssd_dt_bias[0].reshape(-1))
    gate_w, gb = _gla_gate_params(gla_gate_w[0], gla_gate_b[0])
    mixer_args = (w_xbc, ssd_conv_w[0], row(ssd_conv_b[0]), w_small, dtb, gate_w, gb)

    xbc_c, dt_c, la_c, k_c, v_c = _inproj(
        ctx, mc[:, 0], mc[:, 1], g1, bm, *mixer_args, [w_k, w_v],
        ["none"] * 2, [BF16, BF16], tm=ctx.shape[1])
    xbc, dt, la, z, q, k, v, r, gates = _inproj(
        x, mx[:, 0], mx[:, 1], g1, bm, *mixer_args,
        [w_z, w_q, w_k, w_v, w_r, bf(w_merge[0])],
        ["none"] * 5 + ["sigmoid_bias"],
        [BF16] * 6, tm=512)

    y_raw = _ssd(xbc, dt, xbc_c, dt_c, ssd_a_log[0], ssd_d[0])
    o = _gla(q, k, v, la, k_c, v_c, la_c)

    x1, cs, e0, e2 = _merge_up(y_raw, z, o, r, gates, x, mx[:, 2], mx[:, 3], mx[:, 4],
                               row(ssd_norm_g[0]), row(gla_norm_g[0]), row(norm2_g[0]),
                               bf(w_br_ssd[0]), bf(w_br_gla[0]), bf(w_o[0]), bf(w_up[0]),
                               ffn_conv_w[0].reshape(9, -1), tm=FFN_TILE_ROWS * GRID_W)
    return _ffn_down(cs, e0, e2, x1, mx[:, 5], row(ffn_conv_b[0]), bf(w_down[0]), row(final_norm_g))
```

```python
import functools
import math

import numpy as np
import jax
import jax.numpy as jnp
from jax import lax
from jax.experimental import pallas as pl
from jax.experimental.pallas import tpu as pltpu

F32 = jnp.float32
BF16 = jnp.bfloat16

D_MODEL = 1024
GRID_W = 64
CHUNK = 64
EPS = 1e-6
N_MOD = 6

SSD_HEAD_DIM = 64
SSD_HEADS = 16
SSD_GROUPS = 4
SSD_STATE = 128
SSD_GROUP_W = (SSD_HEADS // SSD_GROUPS) * SSD_HEAD_DIM
SSD_D_INNER = D_MODEL

GLA_HEADS = 4
GLA_DK = 128
GLA_DV = 256
GLA_GATE_RANK = 16
GLA_GATE_NORM = 16.0
GLA_KEY_DIM = GLA_HEADS * GLA_DK

D_FF = 2816
SMALL_W = 128
DT_COLS = 2 * SSD_HEADS
NEG_BIG = -1e30
MXU_DIM = 256
SUB_ROWS = 256
UP_CHUNK = 512
SCAN_UNROLL = 8
BLOCK_UNROLL = 4

VMEM_LIMIT_BYTES = 56 * 1024 * 1024


def _cparams(n_grid):
    return pltpu.CompilerParams(
        dimension_semantics=("arbitrary",) * n_grid,
        vmem_limit_bytes=VMEM_LIMIT_BYTES,
    )


def _const_spec(shape):
    nd = len(shape)
    return pl.BlockSpec(tuple(shape), lambda *_: (0,) * nd, pipeline_mode=pl.Buffered(1))


def _silu(v):
    return v * jax.nn.sigmoid(v)


def _softplus(v):
    return jnp.maximum(v, 0.0) + jnp.log1p(jnp.exp(-jnp.abs(v)))


def _split2(v):
    hi = v.astype(BF16)
    lo = (v - hi.astype(F32)).astype(BF16)
    return hi, lo


def _dot(a, b):
    return jnp.dot(a, b, preferred_element_type=F32)


def _dot_tn(a, b):
    return lax.dot_general(a, b, (((0,), (0,)), ((), ())), preferred_element_type=F32)


def _dot_nt(a, b):
    return lax.dot_general(a, b, (((1,), (1,)), ((), ())), preferred_element_type=F32)


def _scan_block_rows(seq, seq_c):
    return math.gcd(math.gcd(seq, seq_c), MXU_DIM)


def _block_tri(rows):
    idx = np.arange(rows)
    same = (idx[:, None] // CHUNK) == (idx[None, :] // CHUNK)
    return jnp.asarray((same & (idx[:, None] >= idx[None, :])).astype(np.float32), BF16)


def _adaln_kernel(c_ref, w_ref, b_ref, o_ref):
    s_hi, s_lo = _split2(_silu(c_ref[...]))
    w_hi, w_lo = _split2(w_ref[...])
    o_ref[...] = _dot(s_hi, w_hi) + _dot(s_lo, w_hi) + _dot(s_hi, w_lo) + b_ref[...]


def _adaln(cond, w, b):
    rows = cond.shape[0]
    n_out = w.shape[1]
    tn = D_MODEL
    return pl.pallas_call(
        _adaln_kernel,
        out_shape=jax.ShapeDtypeStruct((rows, n_out), F32),
        grid=(n_out // tn,),
        in_specs=[
            pl.BlockSpec((rows, D_MODEL), lambda j: (0, 0)),
            pl.BlockSpec((D_MODEL, tn), lambda j: (0, j)),
            pl.BlockSpec((1, tn), lambda j: (0, j)),
        ],
        out_specs=pl.BlockSpec((rows, tn), lambda j: (0, j)),
        compiler_params=_cparams(1),
        name="adaln",
    )(cond, w, b)


HALO_ROWS = 8
EPI_CHUNK = 256


def _inproj_kernel(plain_posts, n_tiles,
                   x_ref, xprev_ref, xnext_ref, shift_ref, scale_ref, g_ref, bias_ref,
                   wxbc_ref, cw_ref, cb_ref, wsm_ref, dtb_ref, gwh_ref, gb_ref,
                   *refs):
    n = len(plain_posts)
    w_refs = refs[:n]
    xbc_o, dt_o, la_o = refs[n:n + 3]
    o_refs = refs[n + 3:]
    i = pl.program_id(1)
    tm = x_ref.shape[1]
    sub = math.gcd(tm, SUB_ROWS)
    n_sub = tm // sub
    hr = HALO_ROWS

    def normmod(xv):
        ms = jnp.mean(xv * xv, axis=-1, keepdims=True)
        return ((xv * lax.rsqrt(ms + EPS)) * g_ref[...]) * (1.0 + scale_ref[0]) + shift_ref[0]

    for s in range(n_sub):
        rows = slice(s * sub, (s + 1) * sub)
        h = normmod(x_ref[0, rows, :])
        hb = h.astype(BF16)

        if s > 0:
            h_prev = normmod(x_ref[0, s * sub - hr:s * sub, :])
        else:
            h_prev = normmod(xprev_ref[0]) * jnp.where(i > 0, 1.0, 0.0)
        if s < n_sub - 1:
            h_next = normmod(x_ref[0, (s + 1) * sub:(s + 1) * sub + hr, :])
        else:
            h_next = normmod(xnext_ref[0]) * jnp.where(i < n_tiles - 1, 1.0, 0.0)
        h_ext = jnp.concatenate([h_prev, h, h_next], axis=0).astype(BF16)
        n_ext = sub + 2 * hr
        view = (sub // 8, 8, EPI_CHUNK)

        def xbc_chunk(cbk):
            cols = slice(cbk * EPI_CHUNK, (cbk + 1) * EPI_CHUNK)
            xe = _dot(h_ext, wxbc_ref[:, cols])
            cur = xe[hr:hr + sub].reshape(view)
            xm1 = pltpu.roll(xe, 1, 0)[hr:hr + sub].reshape(view)
            xp1 = pltpu.roll(xe, n_ext - 1, 0)[hr:hr + sub].reshape(view)
            yv = (cw_ref[0, :, cols][None] * xm1 + cw_ref[1, :, cols][None] * cur
                  + cw_ref[2, :, cols][None] * xp1 + cb_ref[:, cols][None])
            xbc_o[0, rows, cols] = _silu(yv).reshape(sub, EPI_CHUNK).astype(xbc_o.dtype)

        small = _dot(hb, wsm_ref[...])
        dt_o[0, rows, :] = _softplus(small + dtb_ref[...])
        s_hi, s_lo = _split2(small)

        def gate_chunk(cbk):
            cols = slice(cbk * EPI_CHUNK, (cbk + 1) * EPI_CHUNK)
            logit = _dot(s_hi, gwh_ref[:, cols]) + _dot(s_lo, gwh_ref[:, cols]) + gb_ref[:, cols]
            la_o[0, rows, cols] = ((jnp.minimum(logit, 0.0) - jnp.log1p(jnp.exp(-jnp.abs(logit))))
                                   * (1.0 / GLA_GATE_NORM))

        def plain(k):
            acc = _dot(hb, w_refs[k][...])
            if plain_posts[k] == "sigmoid_bias":
                acc = jax.nn.sigmoid(acc + bias_ref[...])
            o_refs[k][0, rows, :] = acc.astype(o_refs[k].dtype)

        epi = ([functools.partial(xbc_chunk, c) for c in range(wxbc_ref.shape[1] // EPI_CHUNK)]
               + [functools.partial(gate_chunk, c) for c in range(gwh_ref.shape[1] // EPI_CHUNK)])
        plains = [functools.partial(plain, k) for k in range(n)]
        for t in range(max(len(epi), len(plains))):
            if t < len(epi):
                epi[t]()
            if t < len(plains):
                plains[t]()


def _inproj(x, shift, scale, g, bias, w_xbc, conv_w, conv_b, w_small, dtb, gw_hi, gb,
            weights, posts, out_dtypes, tm):
    bsz, seq, _ = x.shape
    n_tiles = seq // tm
    hr = HALO_ROWS
    assert tm % hr == 0
    tpb = tm // hr
    n_xbc, n_la = w_xbc.shape[1], gw_hi.shape[1]
    mod_map = (lambda b, i: (b, 0, 0)) if shift.shape[0] > 1 else (lambda b, i: (0, 0, 0))
    conv_w = jnp.broadcast_to(conv_w[:, None, :], (conv_w.shape[0], 8, n_xbc))
    conv_b = jnp.broadcast_to(conv_b, (8, n_xbc))
    in_specs = [
        pl.BlockSpec((1, tm, D_MODEL), lambda b, i: (b, i, 0)),
        pl.BlockSpec((1, hr, D_MODEL), lambda b, i: (b, jnp.maximum(i * tpb - 1, 0), 0)),
        pl.BlockSpec((1, hr, D_MODEL), lambda b, i: (b, jnp.minimum((i + 1) * tpb, seq // hr - 1), 0)),
        pl.BlockSpec((1, 1, D_MODEL), mod_map),
        pl.BlockSpec((1, 1, D_MODEL), mod_map),
        _const_spec((1, D_MODEL)),
        _const_spec(bias.shape),
        _const_spec(w_xbc.shape), _const_spec(conv_w.shape), _const_spec(conv_b.shape),
        _const_spec(w_small.shape), _const_spec(dtb.shape),
        _const_spec(gw_hi.shape), _const_spec(gb.shape),
    ] + [_const_spec(w.shape) for w in weights]
    tok = lambda w: pl.BlockSpec((1, tm, w), lambda b, i: (b, i, 0))
    out_shape = ([jax.ShapeDtypeStruct((bsz, seq, n_xbc), BF16),
                  jax.ShapeDtypeStruct((bsz, seq, SMALL_W), F32),
                  jax.ShapeDtypeStruct((bsz, seq, n_la), F32)]
                 + [jax.ShapeDtypeStruct((bsz, seq, w.shape[1]), dt) for w, dt in zip(weights, out_dtypes)])
    out_specs = [tok(n_xbc), tok(SMALL_W), tok(n_la)] + [tok(w.shape[1]) for w in weights]
    return pl.pallas_call(
        functools.partial(_inproj_kernel, tuple(posts), n_tiles),
        out_shape=out_shape,
        grid=(bsz, n_tiles),
        in_specs=in_specs,
        out_specs=out_specs,
        compiler_params=_cparams(2),
        name="inproj",
    )(x, x, x, shift, scale, g, bias, w_xbc, conv_w, conv_b, w_small, dtb, gw_hi, gb, *weights)


PACK_ROWS = 16


def _bwd_chunk_order(i, nc_c, nct):
    return jnp.where(i < nc_c, nc_c - 1 - i, nct - 1 - (i - nc_c))


def _ssd_kernel(seq, seq_c, blk,
                xs_l, bm_l, cm_l, dt_l, xs_c, bm_c, dt_c,
                e_ref, alog_ref, dskip_ref,
                tri_ref, itile_ref, maskf_ref, maskb_ref, bd_ref,
                y_ref,
                dtf_s, dtb_s, cumf_s, cumb_s,
                sf_s, sb_s, ef_s, eb_s, hf_s, hb_s, hrun_f, hrun_b):
    nc, nc_c = seq // CHUNK, seq_c // CHUNK
    nct = nc + nc_c
    seq_t = seq + seq_c
    gw = SSD_GROUP_W
    cpb = blk // CHUNK

    def expand_rows(dt_ref, dst_off, n_rows):
        def body(i, carry):
            r0 = pl.multiple_of(i * blk, blk)
            dst = pl.ds(pl.multiple_of(dst_off + r0, CHUNK), blk)
            dt = dt_ref[0, pl.ds(r0, blk), :].astype(BF16)
            dtf_s[dst, :] = _dot(dt, e_ref[0, 0])
            dtb_s[dst, :] = _dot(dt, e_ref[0, 1])
            return carry
        lax.fori_loop(0, n_rows // blk, body, 0, unroll=BLOCK_UNROLL)

    expand_rows(dt_c, 0, seq_c)
    expand_rows(dt_l, seq_c, seq)

    a_f = -jnp.exp(alog_ref[0, 0:1, :])
    a_b = -jnp.exp(alog_ref[0, 1:2, :])
    tri = tri_ref[...]

    def cumsums(i, carry):
        r0 = pl.multiple_of(i * blk, blk)
        rows = pl.ds(r0, blk)
        da_b = dtb_s[rows, :] * a_b
        pre = _dot(tri, jnp.concatenate([dtf_s[rows, :] * a_f, da_b], axis=1).astype(BF16))
        cumf_s[rows, :] = pre[:, :gw]
        for c in range(cpb):
            lo_r, hi_r = c * CHUNK, (c + 1) * CHUNK
            tot = pre[hi_r - 1:hi_r, gw:]
            cumb_s[pl.ds(r0 + lo_r, CHUNK), :] = tot - pre[lo_r:hi_r, gw:] + da_b[lo_r:hi_r, :]
        return carry

    lax.fori_loop(0, seq_t // blk, cumsums, 0, unroll=BLOCK_UNROLL)

    def state_rows(xs_ref, bm_ref, chunk_off, n_chunks):
        def body(j, carry):
            rows = pl.ds(pl.multiple_of(j * CHUNK, CHUNK), CHUNK)
            trow = pl.ds(pl.multiple_of((chunk_off + j) * CHUNK, CHUNK), CHUNK)
            cum_f, cum_b = cumf_s[trow, :], cumb_s[trow, :]
            last_f = cum_f[CHUNK - 1:CHUNK, :]
            tot_b = cum_b[0:1, :]
            xs = xs_ref[0, rows, :].astype(F32)
            w_f = (xs * (dtf_s[trow, :] * jnp.exp(last_f - cum_f))).astype(BF16)
            w_b = (xs * (dtb_s[trow, :] * jnp.exp(tot_b - cum_b))).astype(BF16)
            st = _dot_tn(bm_ref[0, rows, :], jnp.concatenate([w_f, w_b], axis=1))
            sf_s[chunk_off + j] = st[:, :gw]
            sb_s[chunk_off + j] = st[:, gw:]
            ef_s[chunk_off + j] = jnp.broadcast_to(jnp.exp(last_f), (8, gw))
            eb_s[chunk_off + j] = jnp.broadcast_to(jnp.exp(tot_b), (8, gw))
            return carry
        lax.fori_loop(0, n_chunks, body, 0, unroll=SCAN_UNROLL)

    state_rows(xs_c, bm_c, 0, nc_c)
    state_rows(xs_l, bm_l, nc_c, nc)

    hrun_f[...] = jnp.zeros_like(hrun_f)
    hrun_b[...] = jnp.zeros_like(hrun_b)

    def recur(i, carry):
        h = hrun_f[...]
        hf_s[i] = h.astype(BF16)
        hrun_f[...] = ef_s[i][0:1, :] * h + sf_s[i]
        jb = _bwd_chunk_order(i, nc_c, nct)
        g = hrun_b[...]
        hb_s[jb] = g.astype(BF16)
        hrun_b[...] = eb_s[jb][0:1, :] * g + sb_s[jb]
        return carry

    lax.fori_loop(0, nct, recur, 0, unroll=BLOCK_UNROLL)

    itile = itile_ref[...]
    dskip = dskip_ref[0]

    def rowvec(v):
        return jnp.sum(v * itile, axis=0, keepdims=True)

    def lat_out(j, carry):
        r0 = pl.multiple_of(j * CHUNK, CHUNK)
        rows = pl.ds(r0, CHUNK)
        trow = pl.ds(pl.multiple_of(seq_c + r0, CHUNK), CHUNK)
        xb = xs_l[0, rows, :]
        bc = bm_l[0, rows, :]
        cc = cm_l[0, rows, :]
        cb4 = _dot_nt(cc, jnp.concatenate([bc, bc, bc, bc], axis=0))
        cum_f, cum_b = cumf_s[trow, :], cumb_s[trow, :]
        lf = jnp.exp(cum_f - rowvec(cum_f) + maskf_ref[...]) * rowvec(dtf_s[trow, :])
        lb = jnp.exp(cum_b - rowvec(cum_b) + maskb_ref[...]) * rowvec(dtb_s[trow, :])
        m = (cb4 * (lf + lb)).astype(BF16)
        xbd = jnp.concatenate([xb, xb, xb, xb], axis=0) * bd_ref[...]
        y = _dot(m, xbd)
        y = y + jnp.exp(cum_f) * _dot(cc, hf_s[nc_c + j]) + jnp.exp(cum_b) * _dot(cc, hb_s[nc_c + j])
        y_ref[0, rows, :] = y + dskip * xb.astype(F32)
        return carry

    lax.fori_loop(0, nc, lat_out, 0, unroll=SCAN_UNROLL)


def _ssd_consts():
    q = CHUNK
    idx = np.arange(q)
    tri = (idx[:, None] >= idx[None, :]).astype(np.float32)
    itile = np.tile(np.eye(q, dtype=np.float32), (1, SSD_GROUP_W // q))
    maskf = np.tile((1.0 - tri) * NEG_BIG, (1, SSD_GROUP_W // q)).astype(np.float32)
    maskb = np.tile((1.0 - tri.T) * NEG_BIG, (1, SSD_GROUP_W // q)).astype(np.float32)
    blk = np.arange(SSD_GROUP_W) // SSD_HEAD_DIM
    bd = (blk[:, None] == blk[None, :]).astype(np.float32)
    e = np.zeros((SSD_GROUPS, 2, SMALL_W, SSD_GROUP_W), np.float32)
    for g in range(SSD_GROUPS):
        for d in range(2):
            for r in range(SSD_GROUPS):
                e[g, d, d * SSD_HEADS + g * 4 + r, r * SSD_HEAD_DIM:(r + 1) * SSD_HEAD_DIM] = 1.0
    return (jnp.asarray(itile), jnp.asarray(maskf), jnp.asarray(maskb),
            jnp.asarray(bd, BF16), jnp.asarray(e, BF16))


def _ssd(xbc_l, dt_l, xbc_c, dt_c, a_log, d_skip):
    bsz, seq, _ = xbc_l.shape
    seq_c = xbc_c.shape[1]
    seq_t = seq + seq_c
    nct = seq_t // CHUNK
    blk = _scan_block_rows(seq, seq_c)
    gw, ns, ng = SSD_GROUP_W, SSD_STATE, SSD_GROUPS
    itile, maskf, maskb, bd, e = _ssd_consts()
    tri = _block_tri(blk)
    alog = jnp.repeat(a_log.reshape(2, ng, 4), SSD_HEAD_DIM, axis=-1).transpose(1, 0, 2)
    dsk = jnp.repeat(d_skip.reshape(ng, 4), SSD_HEAD_DIM, axis=-1).reshape(ng, 1, gw)
    b_off = SSD_D_INNER // ns
    c_off = b_off + ng
    in_specs = [
        pl.BlockSpec((1, seq, gw), lambda b, g: (b, 0, g)),
        pl.BlockSpec((1, seq, ns), lambda b, g: (b, 0, b_off + g)),
        pl.BlockSpec((1, seq, ns), lambda b, g: (b, 0, c_off + g)),
        pl.BlockSpec((1, seq, SMALL_W), lambda b, g: (b, 0, 0)),
        pl.BlockSpec((1, seq_c, gw), lambda b, g: (b, 0, g)),
        pl.BlockSpec((1, seq_c, ns), lambda b, g: (b, 0, b_off + g)),
        pl.BlockSpec((1, seq_c, SMALL_W), lambda b, g: (b, 0, 0)),
        pl.BlockSpec((1, 2, SMALL_W, gw), lambda b, g: (g, 0, 0, 0)),
        pl.BlockSpec((1, 2, gw), lambda b, g: (g, 0, 0)),
        pl.BlockSpec((1, 1, gw), lambda b, g: (g, 0, 0)),
        _const_spec(tri.shape), _const_spec(itile.shape), _const_spec(maskf.shape),
        _const_spec(maskb.shape), _const_spec(bd.shape),
    ]
    scratch = [
        pltpu.VMEM((seq_t, gw), F32),
        pltpu.VMEM((seq_t, gw), F32),
        pltpu.VMEM((seq_t, gw), F32),
        pltpu.VMEM((seq_t, gw), F32),
        pltpu.VMEM((nct, ns, gw), F32),
        pltpu.VMEM((nct, ns, gw), F32),
        pltpu.VMEM((nct, 8, gw), F32),
        pltpu.VMEM((nct, 8, gw), F32),
        pltpu.VMEM((nct, ns, gw), BF16),
        pltpu.VMEM((nct, ns, gw), BF16),
        pltpu.VMEM((ns, gw), F32),
        pltpu.VMEM((ns, gw), F32),
    ]
    return pl.pallas_call(
        functools.partial(_ssd_kernel, seq, seq_c, blk),
        out_shape=jax.ShapeDtypeStruct((bsz, seq, SSD_D_INNER), F32),
        grid=(bsz, ng),
        in_specs=in_specs,
        out_specs=pl.BlockSpec((1, seq, gw), lambda b, g: (b, 0, g)),
        scratch_shapes=scratch,
        compiler_params=_cparams(2),
        name="ssd",
    )(xbc_l, xbc_l, xbc_l, dt_l, xbc_c, xbc_c, dt_c,
      e, alog, dsk, tri, itile, maskf, maskb, bd)


def _gla_kernel(seq, seq_c, blk,
                q_l, k_l, v_l, la_l, k_c, v_c, la_c,
                tri_ref, tril_ref, triu_ref,
                o_ref,
                cum_s, st_s, e_s, hs_s, hrun):
    nc, nc_c = seq // CHUNK, seq_c // CHUNK
    nct = nc + nc_c
    dk = GLA_DK
    cpb = blk // CHUNK
    tri = tri_ref[...]
    scale = GLA_DK ** -0.5
    mid = CHUNK // 2

    def cumsum_rows(la_ref, dst_off, n_rows):
        def body(i, carry):
            r0 = pl.multiple_of(i * blk, blk)
            d0 = pl.multiple_of(dst_off + r0, CHUNK)
            la = la_ref[0, pl.ds(r0, blk), :]
            pre = _dot(tri, la.astype(BF16))
            cum_s[pl.ds(d0, blk), 0:dk] = pre[:, :dk]
            for c in range(cpb):
                lo_r, hi_r = c * CHUNK, (c + 1) * CHUNK
                tot = pre[hi_r - 1:hi_r, dk:]
                cum_s[pl.ds(d0 + lo_r, CHUNK), dk:2 * dk] = tot - pre[lo_r:hi_r, dk:] + la[lo_r:hi_r, dk:]
            return carry
        lax.fori_loop(0, n_rows // blk, body, 0, unroll=BLOCK_UNROLL)

    cumsum_rows(la_c, 0, seq_c)
    cumsum_rows(la_l, seq_c, seq)

    def state_rows(k_ref, v_ref, chunk_off, n_chunks):
        def body(j, carry):
            rows = pl.ds(pl.multiple_of(j * CHUNK, CHUNK), CHUNK)
            trow = pl.ds(pl.multiple_of((chunk_off + j) * CHUNK, CHUNK), CHUNK)
            cum = cum_s[trow, :]
            cum_f, cum_b = cum[:, :dk], cum[:, dk:]
            last_f = cum_f[CHUNK - 1:CHUNK, :]
            tot_b = cum_b[0:1, :]
            k = k_ref[0, rows, :].astype(F32)
            kd = jnp.concatenate([k * jnp.exp(last_f - cum_f), k * jnp.exp(tot_b - cum_b)], axis=1)
            st_s[chunk_off + j] = _dot_tn(v_ref[0, rows, :], kd.astype(BF16))
            e_s[chunk_off + j] = jnp.broadcast_to(
                jnp.concatenate([jnp.exp(last_f), jnp.exp(tot_b)], axis=1), (8, 2 * dk))
            return carry
        lax.fori_loop(0, n_chunks, body, 0, unroll=SCAN_UNROLL)

    state_rows(k_c, v_c, 0, nc_c)
    state_rows(k_l, v_l, nc_c, nc)

    hrun[...] = jnp.zeros_like(hrun)

    def recur(i, carry):
        jb = _bwd_chunk_order(i, nc_c, nct)
        h = hrun[...]
        hs_s[i, :, 0:dk] = h[:, :dk].astype(BF16)
        hs_s[jb, :, dk:2 * dk] = h[:, dk:].astype(BF16)
        dec = jnp.concatenate([e_s[i][0:1, :dk], e_s[jb][0:1, dk:]], axis=1)
        inc = jnp.concatenate([st_s[i, :, 0:dk], st_s[jb, :, dk:2 * dk]], axis=1)
        hrun[...] = dec * h + inc
        return carry

    lax.fori_loop(0, nct, recur, 0, unroll=BLOCK_UNROLL)

    def lat_out(j, carry):
        r0 = pl.multiple_of(j * CHUNK, CHUNK)
        rows = pl.ds(r0, CHUNK)
        trow = pl.ds(pl.multiple_of(seq_c + r0, CHUNK), CHUNK)
        cum = cum_s[trow, :]
        cum_f, cum_b = cum[:, :dk], cum[:, dk:]
        ref_f = cum_f[mid - 1:mid, :]
        ref_b = cum_b[mid:mid + 1, :]
        q = q_l[0, rows, :].astype(F32) * scale
        k = k_l[0, rows, :].astype(F32)
        v = v_l[0, rows, :]
        sc_f = _dot_nt((q * jnp.exp(cum_f - ref_f)).astype(BF16), (k * jnp.exp(ref_f - cum_f)).astype(BF16))
        sc_b = _dot_nt((q * jnp.exp(cum_b - ref_b)).astype(BF16), (k * jnp.exp(ref_b - cum_b)).astype(BF16))
        p = (sc_f * tril_ref[...] + sc_b * triu_ref[...]).astype(BF16)
        qe = jnp.concatenate([q * jnp.exp(cum_f), q * jnp.exp(cum_b)], axis=1).astype(BF16)
        o_ref[0, rows, :] = (_dot(p, v) + _dot_nt(qe, hs_s[nc_c + j])).astype(o_ref.dtype)
        return carry

    lax.fori_loop(0, nc, lat_out, 0, unroll=SCAN_UNROLL)


def _gla_gate_params(gate_w, gate_b):
    dk, nh = GLA_DK, GLA_HEADS
    gwh = gate_w.reshape(2, GLA_GATE_RANK, nh, dk).transpose(1, 2, 0, 3).reshape(GLA_GATE_RANK, nh * 2 * dk)
    gw = jnp.zeros((SMALL_W, nh * 2 * dk), F32).at[DT_COLS:DT_COLS + GLA_GATE_RANK, :].set(gwh)
    gb = gate_b.reshape(2, nh, dk).transpose(1, 0, 2).reshape(1, nh * 2 * dk)
    return gw.astype(BF16), gb


def _gla(q_l, k_l, v_l, la_l, k_c, v_c, la_c):
    bsz, seq, _ = q_l.shape
    seq_c = k_c.shape[1]
    seq_t = seq + seq_c
    nct = seq_t // CHUNK
    blk = _scan_block_rows(seq, seq_c)
    dk, dv, nh = GLA_DK, GLA_DV, GLA_HEADS
    idx = np.arange(CHUNK)
    tril = (idx[:, None] >= idx[None, :]).astype(np.float32)
    in_specs = [
        pl.BlockSpec((1, seq, dk), lambda b, h: (b, 0, h)),
        pl.BlockSpec((1, seq, dk), lambda b, h: (b, 0, h)),
        pl.BlockSpec((1, seq, dv), lambda b, h: (b, 0, h)),
        pl.BlockSpec((1, seq, 2 * dk), lambda b, h: (b, 0, h)),
        pl.BlockSpec((1, seq_c, dk), lambda b, h: (b, 0, h)),
        pl.BlockSpec((1, seq_c, dv), lambda b, h: (b, 0, h)),
        pl.BlockSpec((1, seq_c, 2 * dk), lambda b, h: (b, 0, h)),
        _const_spec((blk, blk)), _const_spec((CHUNK, CHUNK)), _const_spec((CHUNK, CHUNK)),
    ]
    scratch = [
        pltpu.VMEM((seq_t, 2 * dk), F32),
        pltpu.VMEM((nct, dv, 2 * dk), F32),
        pltpu.VMEM((nct, 8, 2 * dk), F32),
        pltpu.VMEM((nct, dv, 2 * dk), BF16),
        pltpu.VMEM((dv, 2 * dk), F32),
    ]
    return pl.pallas_call(
        functools.partial(_gla_kernel, seq, seq_c, blk),
        out_shape=jax.ShapeDtypeStruct((bsz, seq, nh * dv), BF16),
        grid=(bsz, nh),
        in_specs=in_specs,
        out_specs=pl.BlockSpec((1, seq, dv), lambda b, h: (b, 0, h)),
        scratch_shapes=scratch,
        compiler_params=_cparams(2),
        name="gla",
    )(q_l, k_l, v_l, la_l, k_c, v_c, la_c,
      _block_tri(blk), jnp.asarray(tril), jnp.asarray(tril.T))


def _merge_up_kernel(y_ref, z_ref, o_ref, r_ref, gt_ref, x_ref, g1_ref, sh2_ref, sc2_ref,
                     sg_ref, gg_ref, n2_ref, wbs_ref, wbg_ref, wo_ref, wup_ref, cw_ref,
                     x1_ref, cs_ref, e0_ref, e2_ref):
    n_g = y_ref.shape[1] // GRID_W
    n_up = wup_ref.shape[1]

    y = y_ref[0] * _silu(z_ref[0]).astype(F32)
    y = y * lax.rsqrt(jnp.mean(y * y, axis=-1, keepdims=True) + EPS) * sg_ref[...]
    ys = _dot(y.astype(BF16), wbs_ref[...])
    heads = []
    for h in range(GLA_HEADS):
        oh = o_ref[0, :, h * GLA_DV:(h + 1) * GLA_DV].astype(F32)
        heads.append(oh * lax.rsqrt(jnp.mean(oh * oh, axis=-1, keepdims=True) + EPS) * gg_ref[...])
    o = jnp.concatenate(heads, axis=1) * _silu(r_ref[0]).astype(F32)
    os_ = _dot(o.astype(BF16), wbg_ref[...])
    gt = gt_ref[0].astype(F32)
    m = gt[:, :D_MODEL] * ys + gt[:, D_MODEL:] * os_
    out = _dot(m.astype(BF16), wo_ref[...])
    x1 = x_ref[0] + g1_ref[0] * out
    x1_ref[0] = x1
    h2 = x1 * lax.rsqrt(jnp.mean(x1 * x1, axis=-1, keepdims=True) + EPS) * n2_ref[...]
    h2 = (h2 * (1.0 + sc2_ref[0]) + sh2_ref[0]).astype(BF16)

    view = (GRID_W // PACK_ROWS, PACK_ROWS, UP_CHUNK)
    for cbk in range(n_up // UP_CHUNK):
        cols = slice(cbk * UP_CHUNK, (cbk + 1) * UP_CHUNK)
        u = _dot(h2, wup_ref[:, cols])
        c = [[None] * n_g for _ in range(3)]
        for g in range(n_g):
            ug = u[g * GRID_W:(g + 1) * GRID_W]
            taps = (pltpu.roll(ug, 1, 0).astype(BF16).reshape(view),
                    ug.astype(BF16).reshape(view),
                    pltpu.roll(ug, GRID_W - 1, 0).astype(BF16).reshape(view))
            for dr in range(3):
                acc = cw_ref[3 * dr, :, cols].reshape(view) * taps[0]
                acc = acc + cw_ref[3 * dr + 1, :, cols].reshape(view) * taps[1]
                acc = acc + cw_ref[3 * dr + 2, :, cols].reshape(view) * taps[2]
                c[dr][g] = acc.reshape(GRID_W, UP_CHUNK)
        for g in range(n_g):
            part = c[1][g]
            if g > 0:
                part = part + c[0][g - 1]
            if g < n_g - 1:
                part = part + c[2][g + 1]
            cs_ref[0, g * GRID_W:(g + 1) * GRID_W, cols] = part
        e0_ref[0, :, cols] = c[0][n_g - 1]
        e2_ref[0, :, cols] = c[2][0]


def _merge_up(y_raw, z, o, r, gates, x, gate1, shift2, scale2, ssd_g, gla_g, norm2_g,
              w_bs, w_bg, w_o, w_up, conv_w, tm):
    bsz, seq, _ = x.shape
    n_up = w_up.shape[1]
    assert tm % GRID_W == 0 and n_up % UP_CHUNK == 0
    tok_in_row = np.arange(GRID_W)[None, :, None]
    tap_col = (np.arange(conv_w.shape[0]) % 3)[:, None, None]
    keep = ~(((tap_col == 0) & (tok_in_row == 0)) | ((tap_col == 2) & (tok_in_row == GRID_W - 1)))
    conv_w = jnp.where(keep, conv_w.astype(BF16)[:, None, :], 0).astype(BF16)
    tok = lambda w: pl.BlockSpec((1, tm, w), lambda b, i: (b, i, 0))
    mod = pl.BlockSpec((1, 1, D_MODEL), lambda b, i: (b, 0, 0))
    return pl.pallas_call(
        _merge_up_kernel,
        out_shape=[jax.ShapeDtypeStruct((bsz, seq, D_MODEL), F32),
                   jax.ShapeDtypeStruct((bsz, seq, n_up), BF16),
                   jax.ShapeDtypeStruct((bsz, (seq // tm) * GRID_W, n_up), BF16),
                   jax.ShapeDtypeStruct((bsz, (seq // tm) * GRID_W, n_up), BF16)],
        grid=(bsz, seq // tm),
        in_specs=[tok(D_MODEL), tok(D_MODEL), tok(D_MODEL), tok(D_MODEL), tok(2 * D_MODEL), tok(D_MODEL),
                  mod, mod, mod, _const_spec((1, D_MODEL)), _const_spec((1, GLA_DV)),
                  _const_spec((1, D_MODEL)),
                  _const_spec(w_bs.shape), _const_spec(w_bg.shape), _const_spec(w_o.shape),
                  _const_spec(w_up.shape), _const_spec(conv_w.shape)],
        out_specs=[tok(D_MODEL), tok(n_up),
                   pl.BlockSpec((1, GRID_W, n_up), lambda b, i: (b, i, 0)),
                   pl.BlockSpec((1, GRID_W, n_up), lambda b, i: (b, i, 0))],
        compiler_params=_cparams(2),
        name="merge_up",
    )(y_raw, z, o, r, gates, x, gate1, shift2, scale2, ssd_g, gla_g, norm2_g, w_bs, w_bg, w_o, w_up,
      conv_w)


FFN_TILE_ROWS = 4
FFN_CBLK = 256


def _ffn_down_kernel(n_row_tiles,
                     cs_ref, e0_ref, e2_ref, x1_ref, g2_ref, cb_ref, wd_ref, fg_ref,
                     o_ref,
                     act_s):
    i = pl.program_id(1)
    tr, gw, cb = FFN_TILE_ROWS, GRID_W, FFN_CBLK
    top_on = jnp.where(i > 0, 1.0, 0.0)
    bot_on = jnp.where(i < n_row_tiles - 1, 1.0, 0.0)

    def conv_rows(rr, cols):
        v = cs_ref[0, rr * gw:(rr + 1) * gw, cols].astype(F32) + cb_ref[0:1, cols]
        if rr == 0:
            v = v + e0_ref[0, :, cols].astype(F32) * top_on
        if rr == tr - 1:
            v = v + e2_ref[0, :, cols].astype(F32) * bot_on
        return v

    for jb in range(D_FF // cb):
        for rr in range(tr):
            gate = conv_rows(rr, slice(jb * cb, (jb + 1) * cb))
            val = conv_rows(rr, slice(D_FF + jb * cb, D_FF + (jb + 1) * cb))
            act_s[rr * gw:(rr + 1) * gw, jb * cb:(jb + 1) * cb] = (_silu(gate) * val).astype(BF16)

    ffn = _dot(act_s[...], wd_ref[...])
    x2 = x1_ref[0] + g2_ref[0] * ffn
    o_ref[0] = x2 * lax.rsqrt(jnp.mean(x2 * x2, axis=-1, keepdims=True) + EPS) * fg_ref[...]


def _ffn_down(cs, e0, e2, x1, gate2, conv_b, w_down, final_g):
    bsz, seq, n_up = cs.shape
    tr = FFN_TILE_ROWS
    tm = tr * GRID_W
    n_tiles = seq // tm
    return pl.pallas_call(
        functools.partial(_ffn_down_kernel, n_tiles),
        out_shape=jax.ShapeDtypeStruct((bsz, seq, D_MODEL), F32),
        grid=(bsz, n_tiles),
        in_specs=[
            pl.BlockSpec((1, tm, n_up), lambda b, i: (b, i, 0)),
            pl.BlockSpec((1, GRID_W, n_up), lambda b, i: (b, jnp.maximum(i - 1, 0), 0)),
            pl.BlockSpec((1, GRID_W, n_up), lambda b, i: (b, jnp.minimum(i + 1, n_tiles - 1), 0)),
            pl.BlockSpec((1, tm, D_MODEL), lambda b, i: (b, i, 0)),
            pl.BlockSpec((1, 1, D_MODEL), lambda b, i: (b, 0, 0)),
            _const_spec(conv_b.shape),
            _const_spec(w_down.shape), _const_spec((1, D_MODEL)),
        ],
        out_specs=pl.BlockSpec((1, tm, D_MODEL), lambda b, i: (b, i, 0)),
        scratch_shapes=[pltpu.VMEM((tm, D_FF), BF16)],
        compiler_params=_cparams(2),
        name="ffn_down",
    )(cs, e0, e2, x1, gate2, conv_b, w_down, final_g)


def kernel(x, c, ctx, c_ctx, w_ada, b_ada, norm1_g, w_in, ssd_conv_w, ssd_conv_b, ssd_dt_bias,
           ssd_a_log, ssd_d, ssd_norm_g, gla_gate_w, gla_gate_b, gla_norm_g, w_br_ssd, w_br_gla,
           w_merge, b_merge, w_o, norm2_g, w_up, ffn_conv_w, ffn_conv_b, w_down, final_norm_g):
    bsz, seq, _ = x.shape
    assert w_ada.shape[0] == 1, "single-layer trunk"
    row = lambda v: v.reshape(1, -1)

    n_cond = bsz + 1
    pad = (-n_cond) % 8
    cond = jnp.concatenate([c, c_ctx[None], jnp.zeros((pad, D_MODEL), F32)], axis=0)
    mod = _adaln(cond, w_ada[0], row(b_ada[0]))
    mx = mod[:bsz].reshape(bsz, N_MOD, 1, D_MODEL)
    mc = mod[bsz:bsz + 1].reshape(1, N_MOD, 1, D_MODEL)

    wi = w_in[0]
    o_z, o_xbc = 0, SSD_D_INNER
    o_dt = o_xbc + SSD_D_INNER + 2 * SSD_GROUPS * SSD_STATE
    o_q = o_dt + DT_COLS
    o_k = o_q + GLA_KEY_DIM
    o_v = o_k + GLA_KEY_DIM
    o_g = o_v + D_MODEL
    o_r = o_g + GLA_GATE_RANK
    w_small = jnp.zeros((D_MODEL, SMALL_W), F32)
    w_small = w_small.at[:, :DT_COLS].set(wi[:, o_dt:o_q])
    w_small = w_small.at[:, DT_COLS:DT_COLS + GLA_GATE_RANK].set(wi[:, o_g:o_r])
    bf = lambda w: w.astype(BF16)
    w_z, w_xbc = bf(wi[:, o_z:o_xbc]), bf(wi[:, o_xbc:o_dt])
    w_q, w_k, w_v, w_r = bf(wi[:, o_q:o_k]), bf(wi[:, o_k:o_v]), bf(wi[:, o_v:o_g]), bf(wi[:, o_r:])
    w_small = bf(w_small)
    bm = row(b_merge[0])
    g1 = row(norm1_g[0])

    dtb = jnp.zeros((1, SMALL_W), F32).at[0, :DT_COLS].set(ssd_dt_bias[0].reshape(-1))
    gate_w, gb = _gla_gate_params(gla_gate_w[0], gla_gate_b[0])
    mixer_args = (w_xbc, ssd_conv_w[0], row(ssd_conv_b[0]), w_small, dtb, gate_w, gb)

    xbc_c, dt_c, la_c, k_c, v_c = _inproj(
        ctx, mc[:, 0], mc[:, 1], g1, bm, *mixer_args, [w_k, w_v],
        ["none"] * 2, [BF16, BF16], tm=ctx.shape[1])
    xbc, dt, la, z, q, k, v, r, gates = _inproj(
        x, mx[:, 0], mx[:, 1], g1, bm, *mixer_args,
        [w_z, w_q, w_k, w_v, w_r, bf(w_merge[0])],
        ["none"] * 5 + ["sigmoid_bias"],
        [BF16] * 6, tm=512)

    y_raw = _ssd(xbc, dt, xbc_c, dt_c, ssd_a_log[0], ssd_d[0])
    o = _gla(q, k, v, la, k_c, v_c, la_c)

    x1, cs, e0, e2 = _merge_up(y_raw, z, o, r, gates, x, mx[:, 2], mx[:, 3], mx[:, 4],
                               row(ssd_norm_g[0]), row(gla_norm_g[0]), row(norm2_g[0]),
                               bf(w_br_ssd[0]), bf(w_br_gla[0]), bf(w_o[0]), bf(w_up[0]),
                               ffn_conv_w[0].reshape(9, -1), tm=FFN_TILE_ROWS * GRID_W)
    return _ffn_down(cs, e0, e2, x1, mx[:, 5], row(ffn_conv_b[0]), bf(w_down[0]), row(final_norm_g))
```

```python
import functools
import math

import numpy as np
import jax
import jax.numpy as jnp
from jax import lax
from jax.experimental import pallas as pl
from jax.experimental.pallas import tpu as pltpu

F32 = jnp.float32
BF16 = jnp.bfloat16

D_MODEL = 1024
GRID_W = 64
CHUNK = 64
EPS = 1e-6
N_MOD = 6

SSD_HEAD_DIM = 64
SSD_HEADS = 16
SSD_GROUPS = 4
SSD_STATE = 128
SSD_GROUP_W = (SSD_HEADS // SSD_GROUPS) * SSD_HEAD_DIM
SSD_D_INNER = D_MODEL

GLA_HEADS = 4
GLA_DK = 128
GLA_DV = 256
GLA_GATE_RANK = 16
GLA_GATE_NORM = 16.0
GLA_KEY_DIM = GLA_HEADS * GLA_DK

D_FF = 2816
SMALL_W = 128
DT_COLS = 2 * SSD_HEADS
NEG_BIG = -1e30
MXU_DIM = 256
SUB_ROWS = 256
UP_CHUNK = 512
SCAN_UNROLL = 8
BLOCK_UNROLL = 4
OUT_UNROLL = 16

VMEM_LIMIT_BYTES = 56 * 1024 * 1024


def _cparams(n_grid):
    return pltpu.CompilerParams(
        dimension_semantics=("arbitrary",) * n_grid,
        vmem_limit_bytes=VMEM_LIMIT_BYTES,
    )


def _const_spec(shape):
    nd = len(shape)
    return pl.BlockSpec(tuple(shape), lambda *_: (0,) * nd, pipeline_mode=pl.Buffered(1))


def _silu(v):
    return v * jax.nn.sigmoid(v)


def _softplus(v):
    return jnp.maximum(v, 0.0) + jnp.log1p(jnp.exp(-jnp.abs(v)))


def _split2(v):
    hi = v.astype(BF16)
    lo = (v - hi.astype(F32)).astype(BF16)
    return hi, lo


def _dot(a, b):
    return jnp.dot(a, b, preferred_element_type=F32)


def _dot_tn(a, b):
    return lax.dot_general(a, b, (((0,), (0,)), ((), ())), preferred_element_type=F32)


def _dot_nt(a, b):
    return lax.dot_general(a, b, (((1,), (1,)), ((), ())), preferred_element_type=F32)


def _scan_block_rows(seq, seq_c):
    return math.gcd(math.gcd(seq, seq_c), MXU_DIM)


def _block_tri(rows):
    idx = np.arange(rows)
    same = (idx[:, None] // CHUNK) == (idx[None, :] // CHUNK)
    return jnp.asarray((same & (idx[:, None] >= idx[None, :])).astype(np.float32), BF16)


def _adaln_kernel(c_ref, w_ref, b_ref, o_ref):
    s_hi, s_lo = _split2(_silu(c_ref[...]))
    w_hi, w_lo = _split2(w_ref[...])
    o_ref[...] = _dot(s_hi, w_hi) + _dot(s_lo, w_hi) + _dot(s_hi, w_lo) + b_ref[...]


def _adaln(cond, w, b):
    rows = cond.shape[0]
    n_out = w.shape[1]
    tn = D_MODEL
    return pl.pallas_call(
        _adaln_kernel,
        out_shape=jax.ShapeDtypeStruct((rows, n_out), F32),
        grid=(n_out // tn,),
        in_specs=[
            pl.BlockSpec((rows, D_MODEL), lambda j: (0, 0)),
            pl.BlockSpec((D_MODEL, tn), lambda j: (0, j)),
            pl.BlockSpec((1, tn), lambda j: (0, j)),
        ],
        out_specs=pl.BlockSpec((rows, tn), lambda j: (0, j)),
        compiler_params=_cparams(1),
        name="adaln",
    )(cond, w, b)


HALO_ROWS = 8
EPI_CHUNK = 256


def _inproj_kernel(plain_posts, n_tiles,
                   x_ref, xprev_ref, xnext_ref, shift_ref, scale_ref, g_ref, bias_ref,
                   wxbc_ref, cw_ref, cb_ref, wsm_ref, dtb_ref, gwh_ref, gb_ref,
                   *refs):
    n = len(plain_posts)
    w_refs = refs[:n]
    xbc_o, dt_o, la_o = refs[n:n + 3]
    o_refs = refs[n + 3:]
    i = pl.program_id(1)
    tm = x_ref.shape[1]
    sub = math.gcd(tm, SUB_ROWS)
    n_sub = tm // sub
    hr = HALO_ROWS

    def normmod(xv):
        ms = jnp.mean(xv * xv, axis=-1, keepdims=True)
        return ((xv * lax.rsqrt(ms + EPS)) * g_ref[...]) * (1.0 + scale_ref[0]) + shift_ref[0]

    for s in range(n_sub):
        rows = slice(s * sub, (s + 1) * sub)
        h = normmod(x_ref[0, rows, :])
        hb = h.astype(BF16)

        if s > 0:
            h_prev = normmod(x_ref[0, s * sub - hr:s * sub, :])
        else:
            h_prev = normmod(xprev_ref[0]) * jnp.where(i > 0, 1.0, 0.0)
        if s < n_sub - 1:
            h_next = normmod(x_ref[0, (s + 1) * sub:(s + 1) * sub + hr, :])
        else:
            h_next = normmod(xnext_ref[0]) * jnp.where(i < n_tiles - 1, 1.0, 0.0)
        h_ext = jnp.concatenate([h_prev, h, h_next], axis=0).astype(BF16)
        n_ext = sub + 2 * hr
        view = (sub // 8, 8, EPI_CHUNK)

        def xbc_chunk(cbk):
            cols = slice(cbk * EPI_CHUNK, (cbk + 1) * EPI_CHUNK)
            xe = _dot(h_ext, wxbc_ref[:, cols])
            cur = xe[hr:hr + sub].reshape(view)
            xm1 = pltpu.roll(xe, 1, 0)[hr:hr + sub].reshape(view)
            xp1 = pltpu.roll(xe, n_ext - 1, 0)[hr:hr + sub].reshape(view)
            yv = (cw_ref[0, :, cols][None] * xm1 + cw_ref[1, :, cols][None] * cur
                  + cw_ref[2, :, cols][None] * xp1 + cb_ref[:, cols][None])
            xbc_o[0, rows, cols] = _silu(yv).reshape(sub, EPI_CHUNK).astype(xbc_o.dtype)

        small = _dot(hb, wsm_ref[...])
        dt_o[0, rows, :] = _softplus(small + dtb_ref[...])
        s_hl = jnp.concatenate(_split2(small), axis=1)

        def gate_chunk(cbk):
            cols = slice(cbk * EPI_CHUNK, (cbk + 1) * EPI_CHUNK)
            logit = _dot(s_hl, gwh_ref[:, cols]) + gb_ref[:, cols]
            la_o[0, rows, cols] = ((jnp.minimum(logit, 0.0) - jnp.log1p(jnp.exp(-jnp.abs(logit))))
                                   * (1.0 / GLA_GATE_NORM))

        def plain(k):
            acc = _dot(hb, w_refs[k][...])
            if plain_posts[k] == "sigmoid_bias":
                acc = jax.nn.sigmoid(acc + bias_ref[...])
            o_refs[k][0, rows, :] = acc.astype(o_refs[k].dtype)

        epi = ([functools.partial(xbc_chunk, c) for c in range(wxbc_ref.shape[1] // EPI_CHUNK)]
               + [functools.partial(gate_chunk, c) for c in range(gwh_ref.shape[1] // EPI_CHUNK)])
        plains = [functools.partial(plain, k) for k in range(n)]
        for t in range(max(len(epi), len(plains))):
            if t < len(epi):
                epi[t]()
            if t < len(plains):
                plains[t]()


def _inproj(x, shift, scale, g, bias, w_xbc, conv_w, conv_b, w_small, dtb, gw_hi, gb,
            weights, posts, out_dtypes, tm):
    bsz, seq, _ = x.shape
    n_tiles = seq // tm
    hr = HALO_ROWS
    assert tm % hr == 0
    tpb = tm // hr
    n_xbc, n_la = w_xbc.shape[1], gw_hi.shape[1]
    mod_map = (lambda b, i: (b, 0, 0)) if shift.shape[0] > 1 else (lambda b, i: (0, 0, 0))
    conv_w = jnp.broadcast_to(conv_w[:, None, :], (conv_w.shape[0], 8, n_xbc))
    conv_b = jnp.broadcast_to(conv_b, (8, n_xbc))
    in_specs = [
        pl.BlockSpec((1, tm, D_MODEL), lambda b, i: (b, i, 0)),
        pl.BlockSpec((1, hr, D_MODEL), lambda b, i: (b, jnp.maximum(i * tpb - 1, 0), 0)),
        pl.BlockSpec((1, hr, D_MODEL), lambda b, i: (b, jnp.minimum((i + 1) * tpb, seq // hr - 1), 0)),
        pl.BlockSpec((1, 1, D_MODEL), mod_map),
        pl.BlockSpec((1, 1, D_MODEL), mod_map),
        _const_spec((1, D_MODEL)),
        _const_spec(bias.shape),
        _const_spec(w_xbc.shape), _const_spec(conv_w.shape), _const_spec(conv_b.shape),
        _const_spec(w_small.shape), _const_spec(dtb.shape),
        _const_spec(gw_hi.shape), _const_spec(gb.shape),
    ] + [_const_spec(w.shape) for w in weights]
    tok = lambda w: pl.BlockSpec((1, tm, w), lambda b, i: (b, i, 0))
    out_shape = ([jax.ShapeDtypeStruct((bsz, seq, n_xbc), BF16),
                  jax.ShapeDtypeStruct((bsz, seq, SMALL_W), F32),
                  jax.ShapeDtypeStruct((bsz, seq, n_la), F32)]
                 + [jax.ShapeDtypeStruct((bsz, seq, w.shape[1]), dt) for w, dt in zip(weights, out_dtypes)])
    out_specs = [tok(n_xbc), tok(SMALL_W), tok(n_la)] + [tok(w.shape[1]) for w in weights]
    return pl.pallas_call(
        functools.partial(_inproj_kernel, tuple(posts), n_tiles),
        out_shape=out_shape,
        grid=(bsz, n_tiles),
        in_specs=in_specs,
        out_specs=out_specs,
        compiler_params=_cparams(2),
        name="inproj",
    )(x, x, x, shift, scale, g, bias, w_xbc, conv_w, conv_b, w_small, dtb, gw_hi, gb, *weights)


PACK_ROWS = 16


def _bwd_chunk_order(i, nc_c, nct):
    return jnp.where(i < nc_c, nc_c - 1 - i, nct - 1 - (i - nc_c))


def _ssd_kernel(seq, seq_c, blk,
                xs_l, bm_l, cm_l, dt_l, xs_c, bm_c, dt_c,
                e_ref, alog_ref, dskip_ref,
                tri_ref, itile_ref, maskf_ref, maskb_ref, bd_ref,
                y_ref,
                dtf_s, dtb_s, cumf_s, cumb_s,
                sf_s, sb_s, ef_s, eb_s, hf_s, hb_s, hrun_f, hrun_b):
    nc, nc_c = seq // CHUNK, seq_c // CHUNK
    nct = nc + nc_c
    seq_t = seq + seq_c
    gw = SSD_GROUP_W
    cpb = blk // CHUNK

    def expand_rows(dt_ref, dst_off, n_rows):
        def body(i, carry):
            r0 = pl.multiple_of(i * blk, blk)
            dst = pl.ds(pl.multiple_of(dst_off + r0, CHUNK), blk)
            dt = dt_ref[0, pl.ds(r0, blk), :].astype(BF16)
            dtf_s[dst, :] = _dot(dt, e_ref[0, 0])
            dtb_s[dst, :] = _dot(dt, e_ref[0, 1])
            return carry
        lax.fori_loop(0, n_rows // blk, body, 0, unroll=BLOCK_UNROLL)

    expand_rows(dt_c, 0, seq_c)
    expand_rows(dt_l, seq_c, seq)

    a_f = -jnp.exp(alog_ref[0, 0:1, :])
    a_b = -jnp.exp(alog_ref[0, 1:2, :])
    tri = tri_ref[...]

    def cumsums(i, carry):
        r0 = pl.multiple_of(i * blk, blk)
        rows = pl.ds(r0, blk)
        da_b = dtb_s[rows, :] * a_b
        pre = _dot(tri, jnp.concatenate([dtf_s[rows, :] * a_f, da_b], axis=1).astype(BF16))
        cumf_s[rows, :] = pre[:, :gw]
        for c in range(cpb):
            lo_r, hi_r = c * CHUNK, (c + 1) * CHUNK
            tot = pre[hi_r - 1:hi_r, gw:]
            cumb_s[pl.ds(r0 + lo_r, CHUNK), :] = tot - pre[lo_r:hi_r, gw:] + da_b[lo_r:hi_r, :]
        return carry

    lax.fori_loop(0, seq_t // blk, cumsums, 0, unroll=BLOCK_UNROLL)

    def state_rows(xs_ref, bm_ref, chunk_off, n_chunks):
        def body(j, carry):
            rows = pl.ds(pl.multiple_of(j * CHUNK, CHUNK), CHUNK)
            trow = pl.ds(pl.multiple_of((chunk_off + j) * CHUNK, CHUNK), CHUNK)
            cum_f, cum_b = cumf_s[trow, :], cumb_s[trow, :]
            last_f = cum_f[CHUNK - 1:CHUNK, :]
            tot_b = cum_b[0:1, :]
            xs = xs_ref[0, rows, :].astype(F32)
            w_f = (xs * (dtf_s[trow, :] * jnp.exp(last_f - cum_f))).astype(BF16)
            w_b = (xs * (dtb_s[trow, :] * jnp.exp(tot_b - cum_b))).astype(BF16)
            st = _dot_tn(bm_ref[0, rows, :], jnp.concatenate([w_f, w_b], axis=1))
            sf_s[chunk_off + j] = st[:, :gw]
            sb_s[chunk_off + j] = st[:, gw:]
            ef_s[chunk_off + j] = jnp.broadcast_to(jnp.exp(last_f), (8, gw))
            eb_s[chunk_off + j] = jnp.broadcast_to(jnp.exp(tot_b), (8, gw))
            return carry
        lax.fori_loop(0, n_chunks, body, 0, unroll=SCAN_UNROLL)

    state_rows(xs_c, bm_c, 0, nc_c)
    state_rows(xs_l, bm_l, nc_c, nc)

    hrun_f[...] = jnp.zeros_like(hrun_f)
    hrun_b[...] = jnp.zeros_like(hrun_b)

    def recur(i, carry):
        h = hrun_f[...]
        hf_s[i] = h.astype(BF16)
        hrun_f[...] = ef_s[i][0:1, :] * h + sf_s[i]
        jb = _bwd_chunk_order(i, nc_c, nct)
        g = hrun_b[...]
        hb_s[jb] = g.astype(BF16)
        hrun_b[...] = eb_s[jb][0:1, :] * g + sb_s[jb]
        return carry

    lax.fori_loop(0, nct, recur, 0, unroll=BLOCK_UNROLL)

    itile = itile_ref[...]
    dskip = dskip_ref[0]

    def rowvec(v):
        return jnp.sum(v * itile, axis=0, keepdims=True)

    def lat_out(j, carry):
        r0 = pl.multiple_of(j * CHUNK, CHUNK)
        rows = pl.ds(r0, CHUNK)
        trow = pl.ds(pl.multiple_of(seq_c + r0, CHUNK), CHUNK)
        xb = xs_l[0, rows, :]
        bc = bm_l[0, rows, :]
        cc = cm_l[0, rows, :]
        cb4 = _dot_nt(cc, jnp.concatenate([bc, bc, bc, bc], axis=0))
        cum_f, cum_b = cumf_s[trow, :], cumb_s[trow, :]
        lf = jnp.exp(cum_f - rowvec(cum_f) + maskf_ref[...]) * rowvec(dtf_s[trow, :])
        lb = jnp.exp(cum_b - rowvec(cum_b) + maskb_ref[...]) * rowvec(dtb_s[trow, :])
        m = (cb4 * (lf + lb)).astype(BF16)
        xbd = jnp.concatenate([xb, xb, xb, xb], axis=0) * bd_ref[...]
        y = _dot(m, xbd)
        y = y + jnp.exp(cum_f) * _dot(cc, hf_s[nc_c + j]) + jnp.exp(cum_b) * _dot(cc, hb_s[nc_c + j])
        y_ref[0, rows, :] = y + dskip * xb.astype(F32)
        return carry

    lax.fori_loop(0, nc, lat_out, 0, unroll=OUT_UNROLL)


def _ssd_consts():
    q = CHUNK
    idx = np.arange(q)
    tri = (idx[:, None] >= idx[None, :]).astype(np.float32)
    itile = np.tile(np.eye(q, dtype=np.float32), (1, SSD_GROUP_W // q))
    maskf = np.tile((1.0 - tri) * NEG_BIG, (1, SSD_GROUP_W // q)).astype(np.float32)
    maskb = np.tile((1.0 - tri.T) * NEG_BIG, (1, SSD_GROUP_W // q)).astype(np.float32)
    blk = np.arange(SSD_GROUP_W) // SSD_HEAD_DIM
    bd = (blk[:, None] == blk[None, :]).astype(np.float32)
    e = np.zeros((SSD_GROUPS, 2, SMALL_W, SSD_GROUP_W), np.float32)
    for g in range(SSD_GROUPS):
        for d in range(2):
            for r in range(SSD_GROUPS):
                e[g, d, d * SSD_HEADS + g * 4 + r, r * SSD_HEAD_DIM:(r + 1) * SSD_HEAD_DIM] = 1.0
    return (jnp.asarray(itile), jnp.asarray(maskf), jnp.asarray(maskb),
            jnp.asarray(bd, BF16), jnp.asarray(e, BF16))


def _ssd(xbc_l, dt_l, xbc_c, dt_c, a_log, d_skip):
    bsz, seq, _ = xbc_l.shape
    seq_c = xbc_c.shape[1]
    seq_t = seq + seq_c
    nct = seq_t // CHUNK
    blk = _scan_block_rows(seq, seq_c)
    gw, ns, ng = SSD_GROUP_W, SSD_STATE, SSD_GROUPS
    itile, maskf, maskb, bd, e = _ssd_consts()
    tri = _block_tri(blk)
    alog = jnp.repeat(a_log.reshape(2, ng, 4), SSD_HEAD_DIM, axis=-1).transpose(1, 0, 2)
    dsk = jnp.repeat(d_skip.reshape(ng, 4), SSD_HEAD_DIM, axis=-1).reshape(ng, 1, gw)
    b_off = SSD_D_INNER // ns
    c_off = b_off + ng
    in_specs = [
        pl.BlockSpec((1, seq, gw), lambda b, g: (b, 0, g)),
        pl.BlockSpec((1, seq, ns), lambda b, g: (b, 0, b_off + g)),
        pl.BlockSpec((1, seq, ns), lambda b, g: (b, 0, c_off + g)),
        pl.BlockSpec((1, seq, SMALL_W), lambda b, g: (b, 0, 0)),
        pl.BlockSpec((1, seq_c, gw), lambda b, g: (b, 0, g)),
        pl.BlockSpec((1, seq_c, ns), lambda b, g: (b, 0, b_off + g)),
        pl.BlockSpec((1, seq_c, SMALL_W), lambda b, g: (b, 0, 0)),
        pl.BlockSpec((1, 2, SMALL_W, gw), lambda b, g: (g, 0, 0, 0)),
        pl.BlockSpec((1, 2, gw), lambda b, g: (g, 0, 0)),
        pl.BlockSpec((1, 1, gw), lambda b, g: (g, 0, 0)),
        _const_spec(tri.shape), _const_spec(itile.shape), _const_spec(maskf.shape),
        _const_spec(maskb.shape), _const_spec(bd.shape),
    ]
    scratch = [
        pltpu.VMEM((seq_t, gw), F32),
        pltpu.VMEM((seq_t, gw), F32),
        pltpu.VMEM((seq_t, gw), F32),
        pltpu.VMEM((seq_t, gw), F32),
        pltpu.VMEM((nct, ns, gw), F32),
        pltpu.VMEM((nct, ns, gw), F32),
        pltpu.VMEM((nct, 8, gw), F32),
        pltpu.VMEM((nct, 8, gw), F32),
        pltpu.VMEM((nct, ns, gw), BF16),
        pltpu.VMEM((nct, ns, gw), BF16),
        pltpu.VMEM((ns, gw), F32),
        pltpu.VMEM((ns, gw), F32),
    ]
    return pl.pallas_call(
        functools.partial(_ssd_kernel, seq, seq_c, blk),
        out_shape=jax.ShapeDtypeStruct((bsz, seq, SSD_D_INNER), F32),
        grid=(bsz, ng),
        in_specs=in_specs,
        out_specs=pl.BlockSpec((1, seq, gw), lambda b, g: (b, 0, g)),
        scratch_shapes=scratch,
        compiler_params=_cparams(2),
        name="ssd",
    )(xbc_l, xbc_l, xbc_l, dt_l, xbc_c, xbc_c, dt_c,
      e, alog, dsk, tri, itile, maskf, maskb, bd)


def _gla_kernel(seq, seq_c, blk,
                q_l, k_l, v_l, la_l, k_c, v_c, la_c,
                tri_ref, tril_ref, triu_ref,
                o_ref,
                cum_s, st_s, e_s, hs_s, hrun):
    nc, nc_c = seq // CHUNK, seq_c // CHUNK
    nct = nc + nc_c
    dk = GLA_DK
    cpb = blk // CHUNK
    tri = tri_ref[...]
    scale = GLA_DK ** -0.5
    mid = CHUNK // 2

    def cumsum_rows(la_ref, dst_off, n_rows):
        def body(i, carry):
            r0 = pl.multiple_of(i * blk, blk)
            d0 = pl.multiple_of(dst_off + r0, CHUNK)
            la = la_ref[0, pl.ds(r0, blk), :]
            pre = _dot(tri, la.astype(BF16))
            cum_s[pl.ds(d0, blk), 0:dk] = pre[:, :dk]
            for c in range(cpb):
                lo_r, hi_r = c * CHUNK, (c + 1) * CHUNK
                tot = pre[hi_r - 1:hi_r, dk:]
                cum_s[pl.ds(d0 + lo_r, CHUNK), dk:2 * dk] = tot - pre[lo_r:hi_r, dk:] + la[lo_r:hi_r, dk:]
            return carry
        lax.fori_loop(0, n_rows // blk, body, 0, unroll=BLOCK_UNROLL)

    cumsum_rows(la_c, 0, seq_c)
    cumsum_rows(la_l, seq_c, seq)

    def state_rows(k_ref, v_ref, chunk_off, n_chunks):
        def body(j, carry):
            rows = pl.ds(pl.multiple_of(j * CHUNK, CHUNK), CHUNK)
            trow = pl.ds(pl.multiple_of((chunk_off + j) * CHUNK, CHUNK), CHUNK)
            cum = cum_s[trow, :]
            cum_f, cum_b = cum[:, :dk], cum[:, dk:]
            last_f = cum_f[CHUNK - 1:CHUNK, :]
            tot_b = cum_b[0:1, :]
            k = k_ref[0, rows, :].astype(F32)
            kd = jnp.concatenate([k * jnp.exp(last_f - cum_f), k * jnp.exp(tot_b - cum_b)], axis=1)
            st_s[chunk_off + j] = _dot_tn(v_ref[0, rows, :], kd.astype(BF16))
            e_s[chunk_off + j] = jnp.broadcast_to(
                jnp.concatenate([jnp.exp(last_f), jnp.exp(tot_b)], axis=1), (8, 2 * dk))
            return carry
        lax.fori_loop(0, n_chunks, body, 0, unroll=SCAN_UNROLL)

    state_rows(k_c, v_c, 0, nc_c)
    state_rows(k_l, v_l, nc_c, nc)

    hrun[...] = jnp.zeros_like(hrun)

    def recur(i, carry):
        jb = _bwd_chunk_order(i, nc_c, nct)
        h = hrun[...]
        hs_s[i, :, 0:dk] = h[:, :dk].astype(BF16)
        hs_s[jb, :, dk:2 * dk] = h[:, dk:].astype(BF16)
        dec = jnp.concatenate([e_s[i][0:1, :dk], e_s[jb][0:1, dk:]], axis=1)
        inc = jnp.concatenate([st_s[i, :, 0:dk], st_s[jb, :, dk:2 * dk]], axis=1)
        hrun[...] = dec * h + inc
        return carry

    lax.fori_loop(0, nct, recur, 0, unroll=BLOCK_UNROLL)

    def lat_out(j, carry):
        r0 = pl.multiple_of(j * CHUNK, CHUNK)
        rows = pl.ds(r0, CHUNK)
        trow = pl.ds(pl.multiple_of(seq_c + r0, CHUNK), CHUNK)
        cum = cum_s[trow, :]
        cum_f, cum_b = cum[:, :dk], cum[:, dk:]
        ref_f = cum_f[mid - 1:mid, :]
        ref_b = cum_b[mid:mid + 1, :]
        q = q_l[0, rows, :].astype(F32) * scale
        k = k_l[0, rows, :].astype(F32)
        v = v_l[0, rows, :]
        sc_f = _dot_nt((q * jnp.exp(cum_f - ref_f)).astype(BF16), (k * jnp.exp(ref_f - cum_f)).astype(BF16))
        sc_b = _dot_nt((q * jnp.exp(cum_b - ref_b)).astype(BF16), (k * jnp.exp(ref_b - cum_b)).astype(BF16))
        p = (sc_f * tril_ref[...] + sc_b * triu_ref[...]).astype(BF16)
        qe = jnp.concatenate([q * jnp.exp(cum_f), q * jnp.exp(cum_b)], axis=1).astype(BF16)
        o_ref[0, rows, :] = (_dot(p, v) + _dot_nt(qe, hs_s[nc_c + j])).astype(o_ref.dtype)
        return carry

    lax.fori_loop(0, nc, lat_out, 0, unroll=OUT_UNROLL)


def _gla_gate_params(gate_w, gate_b):
    dk, nh = GLA_DK, GLA_HEADS
    gwh = gate_w.reshape(2, GLA_GATE_RANK, nh, dk).transpose(1, 2, 0, 3).reshape(GLA_GATE_RANK, nh * 2 * dk)
    gw = jnp.zeros((SMALL_W, nh * 2 * dk), F32).at[DT_COLS:DT_COLS + GLA_GATE_RANK, :].set(gwh)
    gb = gate_b.reshape(2, nh, dk).transpose(1, 0, 2).reshape(1, nh * 2 * dk)
    return jnp.concatenate([gw, gw], axis=0).astype(BF16), gb


def _gla(q_l, k_l, v_l, la_l, k_c, v_c, la_c):
    bsz, seq, _ = q_l.shape
    seq_c = k_c.shape[1]
    seq_t = seq + seq_c
    nct = seq_t // CHUNK
    blk = _scan_block_rows(seq, seq_c)
    dk, dv, nh = GLA_DK, GLA_DV, GLA_HEADS
    idx = np.arange(CHUNK)
    tril = (idx[:, None] >= idx[None, :]).astype(np.float32)
    in_specs = [
        pl.BlockSpec((1, seq, dk), lambda b, h: (b, 0, h)),
        pl.BlockSpec((1, seq, dk), lambda b, h: (b, 0, h)),
        pl.BlockSpec((1, seq, dv), lambda b, h: (b, 0, h)),
        pl.BlockSpec((1, seq, 2 * dk), lambda b, h: (b, 0, h)),
        pl.BlockSpec((1, seq_c, dk), lambda b, h: (b, 0, h)),
        pl.BlockSpec((1, seq_c, dv), lambda b, h: (b, 0, h)),
        pl.BlockSpec((1, seq_c, 2 * dk), lambda b, h: (b, 0, h)),
        _const_spec((blk, blk)), _const_spec((CHUNK, CHUNK)), _const_spec((CHUNK, CHUNK)),
    ]
    scratch = [
        pltpu.VMEM((seq_t, 2 * dk), F32),
        pltpu.VMEM((nct, dv, 2 * dk), F32),
        pltpu.VMEM((nct, 8, 2 * dk), F32),
        pltpu.VMEM((nct, dv, 2 * dk), BF16),
        pltpu.VMEM((dv, 2 * dk), F32),
    ]
    return pl.pallas_call(
        functools.partial(_gla_kernel, seq, seq_c, blk),
        out_shape=jax.ShapeDtypeStruct((bsz, seq, nh * dv), BF16),
        grid=(bsz, nh),
        in_specs=in_specs,
        out_specs=pl.BlockSpec((1, seq, dv), lambda b, h: (b, 0, h)),
        scratch_shapes=scratch,
        compiler_params=_cparams(2),
        name="gla",
    )(q_l, k_l, v_l, la_l, k_c, v_c, la_c,
      _block_tri(blk), jnp.asarray(tril), jnp.asarray(tril.T))


def _merge_up_kernel(y_ref, z_ref, o_ref, r_ref, gt_ref, x_ref, g1_ref, sh2_ref, sc2_ref,
                     sg_ref, gg_ref, n2_ref, wbs_ref, wbg_ref, wo_ref, wup_ref, cw_ref,
                     x1_ref, cs_ref, e0_ref, e2_ref):
    n_g = y_ref.shape[1] // GRID_W
    n_up = wup_ref.shape[1]

    y = y_ref[0] * _silu(z_ref[0]).astype(F32)
    y = y * lax.rsqrt(jnp.mean(y * y, axis=-1, keepdims=True) + EPS) * sg_ref[...]
    ys = _dot(y.astype(BF16), wbs_ref[...])
    heads = []
    for h in range(GLA_HEADS):
        oh = o_ref[0, :, h * GLA_DV:(h + 1) * GLA_DV].astype(F32)
        heads.append(oh * lax.rsqrt(jnp.mean(oh * oh, axis=-1, keepdims=True) + EPS) * gg_ref[...])
    o = jnp.concatenate(heads, axis=1) * _silu(r_ref[0]).astype(F32)
    os_ = _dot(o.astype(BF16), wbg_ref[...])
    gt = gt_ref[0].astype(F32)
    m = gt[:, :D_MODEL] * ys + gt[:, D_MODEL:] * os_
    out = _dot(m.astype(BF16), wo_ref[...])
    x1 = x_ref[0] + g1_ref[0] * out
    x1_ref[0] = x1
    h2 = x1 * lax.rsqrt(jnp.mean(x1 * x1, axis=-1, keepdims=True) + EPS) * n2_ref[...]
    h2 = (h2 * (1.0 + sc2_ref[0]) + sh2_ref[0]).astype(BF16)

    view = (GRID_W // PACK_ROWS, PACK_ROWS, UP_CHUNK)
    for cbk in range(n_up // UP_CHUNK):
        cols = slice(cbk * UP_CHUNK, (cbk + 1) * UP_CHUNK)
        u = _dot(h2, wup_ref[:, cols])
        c = [[None] * n_g for _ in range(3)]
        for g in range(n_g):
            ug = u[g * GRID_W:(g + 1) * GRID_W]
            taps = (pltpu.roll(ug, 1, 0).astype(BF16).reshape(view),
                    ug.astype(BF16).reshape(view),
                    pltpu.roll(ug, GRID_W - 1, 0).astype(BF16).reshape(view))
            for dr in range(3):
                acc = cw_ref[3 * dr, :, cols].reshape(view) * taps[0]
                acc = acc + cw_ref[3 * dr + 1, :, cols].reshape(view) * taps[1]
                acc = acc + cw_ref[3 * dr + 2, :, cols].reshape(view) * taps[2]
                c[dr][g] = acc.reshape(GRID_W, UP_CHUNK)
        for g in range(n_g):
            part = c[1][g]
            if g > 0:
                part = part + c[0][g - 1]
            if g < n_g - 1:
                part = part + c[2][g + 1]
            cs_ref[0, g * GRID_W:(g + 1) * GRID_W, cols] = part
        e0_ref[0, :, cols] = c[0][n_g - 1]
        e2_ref[0, :, cols] = c[2][0]


def _merge_up(y_raw, z, o, r, gates, x, gate1, shift2, scale2, ssd_g, gla_g, norm2_g,
              w_bs, w_bg, w_o, w_up, conv_w, tm):
    bsz, seq, _ = x.shape
    n_up = w_up.shape[1]
    assert tm % GRID_W == 0 and n_up % UP_CHUNK == 0
    tok_in_row = np.arange(GRID_W)[None, :, None]
    tap_col = (np.arange(conv_w.shape[0]) % 3)[:, None, None]
    keep = ~(((tap_col == 0) & (tok_in_row == 0)) | ((tap_col == 2) & (tok_in_row == GRID_W - 1)))
    conv_w = jnp.where(keep, conv_w.astype(BF16)[:, None, :], 0).astype(BF16)
    tok = lambda w: pl.BlockSpec((1, tm, w), lambda b, i: (b, i, 0))
    mod = pl.BlockSpec((1, 1, D_MODEL), lambda b, i: (b, 0, 0))
    return pl.pallas_call(
        _merge_up_kernel,
        out_shape=[jax.ShapeDtypeStruct((bsz, seq, D_MODEL), F32),
                   jax.ShapeDtypeStruct((bsz, seq, n_up), BF16),
                   jax.ShapeDtypeStruct((bsz, (seq // tm) * GRID_W, n_up), BF16),
                   jax.ShapeDtypeStruct((bsz, (seq // tm) * GRID_W, n_up), BF16)],
        grid=(bsz, seq // tm),
        in_specs=[tok(D_MODEL), tok(D_MODEL), tok(D_MODEL), tok(D_MODEL), tok(2 * D_MODEL), tok(D_MODEL),
                  mod, mod, mod, _const_spec((1, D_MODEL)), _const_spec((1, GLA_DV)),
                  _const_spec((1, D_MODEL)),
                  _const_spec(w_bs.shape), _const_spec(w_bg.shape), _const_spec(w_o.shape),
                  _const_spec(w_up.shape), _const_spec(conv_w.shape)],
        out_specs=[tok(D_MODEL), tok(n_up),
                   pl.BlockSpec((1, GRID_W, n_up), lambda b, i: (b, i, 0)),
                   pl.BlockSpec((1, GRID_W, n_up), lambda b, i: (b, i, 0))],
        compiler_params=_cparams(2),
        name="merge_up",
    )(y_raw, z, o, r, gates, x, gate1, shift2, scale2, ssd_g, gla_g, norm2_g, w_bs, w_bg, w_o, w_up,
      conv_w)


FFN_TILE_ROWS = 4
FFN_CBLK = 256


def _ffn_down_kernel(n_row_tiles,
                     cs_ref, e0_ref, e2_ref, x1_ref, g2_ref, cb_ref, wd_ref, fg_ref,
                     o_ref,
                     act_s):
    i = pl.program_id(1)
    tr, gw, cb = FFN_TILE_ROWS, GRID_W, FFN_CBLK
    top_on = jnp.where(i > 0, 1.0, 0.0)
    bot_on = jnp.where(i < n_row_tiles - 1, 1.0, 0.0)

    def conv_rows(rr, cols):
        v = cs_ref[0, rr * gw:(rr + 1) * gw, cols].astype(F32) + cb_ref[0:1, cols]
        if rr == 0:
            v = v + e0_ref[0, :, cols].astype(F32) * top_on
        if rr == tr - 1:
            v = v + e2_ref[0, :, cols].astype(F32) * bot_on
        return v

    for jb in range(D_FF // cb):
        for rr in range(tr):
            gate = conv_rows(rr, slice(jb * cb, (jb + 1) * cb))
            val = conv_rows(rr, slice(D_FF + jb * cb, D_FF + (jb + 1) * cb))
            act_s[rr * gw:(rr + 1) * gw, jb * cb:(jb + 1) * cb] = (_silu(gate) * val).astype(BF16)

    ffn = _dot(act_s[...], wd_ref[...])
    x2 = x1_ref[0] + g2_ref[0] * ffn
    o_ref[0] = x2 * lax.rsqrt(jnp.mean(x2 * x2, axis=-1, keepdims=True) + EPS) * fg_ref[...]


def _ffn_down(cs, e0, e2, x1, gate2, conv_b, w_down, final_g):
    bsz, seq, n_up = cs.shape
    tr = FFN_TILE_ROWS
    tm = tr * GRID_W
    n_tiles = seq // tm
    return pl.pallas_call(
        functools.partial(_ffn_down_kernel, n_tiles),
        out_shape=jax.ShapeDtypeStruct((bsz, seq, D_MODEL), F32),
        grid=(bsz, n_tiles),
        in_specs=[
            pl.BlockSpec((1, tm, n_up), lambda b, i: (b, i, 0)),
            pl.BlockSpec((1, GRID_W, n_up), lambda b, i: (b, jnp.maximum(i - 1, 0), 0)),
            pl.BlockSpec((1, GRID_W, n_up), lambda b, i: (b, jnp.minimum(i + 1, n_tiles - 1), 0)),
            pl.BlockSpec((1, tm, D_MODEL), lambda b, i: (b, i, 0)),
            pl.BlockSpec((1, 1, D_MODEL), lambda b, i: (b, 0, 0)),
            _const_spec(conv_b.shape),
            _const_spec(w_down.shape), _const_spec((1, D_MODEL)),
        ],
        out_specs=pl.BlockSpec((1, tm, D_MODEL), lambda b, i: (b, i, 0)),
        scratch_shapes=[pltpu.VMEM((tm, D_FF), BF16)],
        compiler_params=_cparams(2),
        name="ffn_down",
    )(cs, e0, e2, x1, gate2, conv_b, w_down, final_g)


def kernel(x, c, ctx, c_ctx, w_ada, b_ada, norm1_g, w_in, ssd_conv_w, ssd_conv_b, ssd_dt_bias,
           ssd_a_log, ssd_d, ssd_norm_g, gla_gate_w, gla_gate_b, gla_norm_g, w_br_ssd, w_br_gla,
           w_merge, b_merge, w_o, norm2_g, w_up, ffn_conv_w, ffn_conv_b, w_down, final_norm_g):
    bsz, seq, _ = x.shape
    assert w_ada.shape[0] == 1, "single-layer trunk"
    row = lambda v: v.reshape(1, -1)

    n_cond = bsz + 1
    pad = (-n_cond) % 8
    cond = jnp.concatenate([c, c_ctx[None], jnp.zeros((pad, D_MODEL), F32)], axis=0)
    mod = _adaln(cond, w_ada[0], row(b_ada[0]))
    mx = mod[:bsz].reshape(bsz, N_MOD, 1, D_MODEL)
    mc = mod[bsz:bsz + 1].reshape(1, N_MOD, 1, D_MODEL)

    wi = w_in[0]
    o_z, o_xbc = 0, SSD_D_INNER
    o_dt = o_xbc + SSD_D_INNER + 2 * SSD_GROUPS * SSD_STATE
    o_q = o_dt + DT_COLS
    o_k = o_q + GLA_KEY_DIM
    o_v = o_k + GLA_KEY_DIM
    o_g = o_v + D_MODEL
    o_r = o_g + GLA_GATE_RANK
    w_small = jnp.zeros((D_MODEL, SMALL_W), F32)
    w_small = w_small.at[:, :DT_COLS].set(wi[:, o_dt:o_q])
    w_small = w_small.at[:, DT_COLS:DT_COLS + GLA_GATE_RANK].set(wi[:, o_g:o_r])
    bf = lambda w: w.astype(BF16)
    w_z, w_xbc = bf(wi[:, o_z:o_xbc]), bf(wi[:, o_xbc:o_dt])
    w_q, w_k, w_v, w_r = bf(wi[:, o_q:o_k]), bf(wi[:, o_k:o_v]), bf(wi[:, o_v:o_g]), bf(wi[:, o_r:])
    w_small = bf(w_small)
    bm = row(b_merge[0])
    g1 = row(norm1_g[0])

    dtb = jnp.zeros((1, SMALL_W), F32).at[0, :DT_COLS].set(ssd_dt_bias[0].reshape(-1))
    gate_w, gb = _gla_gate_params(gla_gate_w[0], gla_gate_b[0])
    mixer_args = (w_xbc, ssd_conv_w[0], row(ssd_conv_b[0]), w_small, dtb, gate_w, gb)

    xbc_c, dt_c, la_c, k_c, v_c = _inproj(
        ctx, mc[:, 0], mc[:, 1], g1, bm, *mixer_args, [w_k, w_v],
        ["none"] * 2, [BF16, BF16], tm=ctx.shape[1])
    xbc, dt, la, z, q, k, v, r, gates = _inproj(
        x, mx[:, 0], mx[:, 1], g1, bm, *mixer_args,
        [w_z, w_q, w_k, w_v, w_r, bf(w_merge[0])],
        ["none"] * 5 + ["sigmoid_bias"],
        [BF16] * 6, tm=512)

    y_raw = _ssd(xbc, dt, xbc_c, dt_c, ssd_a_log[0], ssd_d[0])
    o = _gla(q, k, v, la, k_c, v_c, la_c)

    x1, cs, e0, e2 = _merge_up(y_raw, z, o, r, gates, x, mx[:, 2], mx[:, 3], mx[:, 4],
                               row(ssd_norm_g[0]), row(gla_norm_g[0]), row(norm2_g[0]),
                               bf(w_br_ssd[0]), bf(w_br_gla[0]), bf(w_o[0]), bf(w_up[0]),
                               ffn_conv_w[0].reshape(9, -1), tm=FFN_TILE_ROWS * GRID_W)
    return _ffn_down(cs, e0, e2, x1, mx[:, 5], row(ffn_conv_b[0]), bf(w_down[0]), row(final_norm_g))
```
